```python
import math
import jax, jax.numpy as jnp
from jax import lax
import numpy as np

D_MODEL = 2048
BATCH = 8
SEQ = 2048
DEPTH = 1

CHUNK = 64
Q_BLOCK = 128
D_MIX = D_MODEL
CONV_CH = D_MIX // 2
CONV_K = 31
ATT_HEADS = 8
ATT_QK_DIM = 64
ATT_V_DIM = 2 * ATT_QK_DIM
ATT_QK = ATT_HEADS * 2 * ATT_QK_DIM
ATT_V = ATT_HEADS * ATT_V_DIM
D_IN = 2 * CONV_CH + 2 * ATT_QK + ATT_V
D_FF = 5632
FFN_CONV_K = 3
NORM_EPS = 1e-6
LN_EPS = 1e-5

kernel_name = "hybrid_conformer_conv_diff_attn_block"


def rmsnorm(x, g):
    xf = x.astype(jnp.float32)
    y = xf * lax.rsqrt(jnp.mean(xf * xf, axis=-1, keepdims=True) + NORM_EPS)
    return (y * g.astype(jnp.float32)).astype(x.dtype)


def layernorm(x, g, b):
    xf = x.astype(jnp.float32)
    mu = jnp.mean(xf, axis=-1, keepdims=True)
    var = jnp.mean(jnp.square(xf - mu), axis=-1, keepdims=True)
    y = (xf - mu) * lax.rsqrt(var + LN_EPS)
    return (y * g.astype(jnp.float32) + b.astype(jnp.float32)).astype(x.dtype)


def causal_dwconv(x, w, b):
    k = w.shape[0]
    y = lax.conv_general_dilated(
        x, w.astype(x.dtype)[:, None, :], window_strides=(1,),
        padding=[(k - 1, 0)], dimension_numbers=("NWC", "WIO", "NWC"),
        feature_group_count=x.shape[-1])
    return y + b.astype(x.dtype)


def diff_attention(q, k, v, lam):
    seq = q.shape[1]
    scale = 1.0 / math.sqrt(ATT_QK_DIM)
    outs = []
    for i in range(seq // Q_BLOCK):
        q0 = i * Q_BLOCK
        kv_len = q0 + Q_BLOCK
        qs = q[:, q0:kv_len]
        ks = k[:, :kv_len]
        vs = v[:, :kv_len]
        s = jnp.einsum("bqhmd,bkhmd->bhmqk", qs, ks,
                       preferred_element_type=jnp.float32) * scale
        q_chunk = (q0 + jnp.arange(Q_BLOCK)) // CHUNK
        k_chunk = jnp.arange(kv_len) // CHUNK
        allowed = k_chunk[None, :] <= q_chunk[:, None]
        s = jnp.where(allowed, s, jnp.float32(-1e30))
        p = jax.nn.softmax(s, axis=-1)
        a = p[:, :, 0] - lam * p[:, :, 1]
        outs.append(jnp.einsum("bhqk,bkhd->bqhd", a.astype(v.dtype), vs))
    return jnp.concatenate(outs, axis=1)


def hybrid_layer(x, layer_idx, norm_mix_g, w_in, conv_dw_w, conv_dw_b, conv_ln_g,
                 conv_ln_b, lambda_q1, lambda_k1, lambda_q2, lambda_k2, subln_g,
                 w_out, norm_ffn_g, w_up, ffn_dw_w, ffn_dw_b, w_down):
    bsz, seq, _ = x.shape
    h = rmsnorm(x, norm_mix_g)
    z = h @ w_in
    a, g, q, k, v = jnp.split(
        z, [CONV_CH, 2 * CONV_CH, 2 * CONV_CH + ATT_QK, 2 * CONV_CH + 2 * ATT_QK], axis=-1)

    u = a * jax.nn.sigmoid(g)
    u = causal_dwconv(u, conv_dw_w, conv_dw_b)
    u = layernorm(u, conv_ln_g, conv_ln_b)
    conv_out = jax.nn.silu(u)

    lambda_init = 0.8 - 0.6 * math.exp(-0.3 * layer_idx)
    lam = (jnp.exp(jnp.sum(lambda_q1.astype(jnp.float32) * lambda_k1.astype(jnp.float32)))
           - jnp.exp(jnp.sum(lambda_q2.astype(jnp.float32) * lambda_k2.astype(jnp.float32)))
           + lambda_init)
    q = q.reshape(bsz, seq, ATT_HEADS, 2, ATT_QK_DIM)
    k = k.reshape(bsz, seq, ATT_HEADS, 2, ATT_QK_DIM)
    v = v.reshape(bsz, seq, ATT_HEADS, ATT_V_DIM)
    o = diff_attention(q, k, v, lam)
    o = rmsnorm(o, subln_g) * (1.0 - lambda_init)
    attn_out = o.reshape(bsz, seq, ATT_V)

    mix = jnp.concatenate([conv_out, attn_out], axis=-1) @ w_out
    x = x + mix

    h = rmsnorm(x, norm_ffn_g)
    gate, up = jnp.split(h @ w_up, 2, axis=-1)
    gate = causal_dwconv(gate, ffn_dw_w, ffn_dw_b)
    x = x + (jax.nn.gelu(gate, approximate=False) * up) @ w_down
    return x


def setup_inputs(seed: int = 0) -> dict:
    key = jax.random.key(seed)
    ks = jax.random.split(key, 20)
    f32 = jnp.float32
    L = DEPTH

    def nrm(k, shape, scale):
        return jax.random.normal(k, shape, f32) * scale

    return {
        "x": nrm(ks[0], (BATCH, SEQ, D_MODEL), 1.0),
        "norm_mix_g": 1.0 + nrm(ks[1], (L, D_MODEL), 0.02),
        "w_in": nrm(ks[2], (L, D_MODEL, D_IN), D_MODEL ** -0.5),
        "conv_dw_w": nrm(ks[3], (L, CONV_K, CONV_CH), CONV_K ** -0.5),
        "conv_dw_b": nrm(ks[4], (L, CONV_CH), 0.02),
        "conv_ln_g": 1.0 + nrm(ks[5], (L, CONV_CH), 0.02),
        "conv_ln_b": nrm(ks[6], (L, CONV_CH), 0.02),
        "lambda_q1": nrm(ks[7], (L, ATT_QK_DIM), 0.1),
        "lambda_k1": nrm(ks[8], (L, ATT_QK_DIM), 0.1),
        "lambda_q2": nrm(ks[9], (L, ATT_QK_DIM), 0.1),
        "lambda_k2": nrm(ks[10], (L, ATT_QK_DIM), 0.1),
        "subln_g": 1.0 + nrm(ks[11], (L, ATT_V_DIM), 0.02),
        "w_out": nrm(ks[12], (L, D_MIX, D_MODEL), D_MIX ** -0.5),
        "norm_ffn_g": 1.0 + nrm(ks[13], (L, D_MODEL), 0.02),
        "w_up": nrm(ks[14], (L, D_MODEL, 2 * D_FF), D_MODEL ** -0.5),
        "ffn_dw_w": nrm(ks[15], (L, FFN_CONV_K, D_FF), FFN_CONV_K ** -0.5),
        "ffn_dw_b": nrm(ks[16], (L, D_FF), 0.02),
        "w_down": nrm(ks[17], (L, D_FF, D_MODEL), D_FF ** -0.5),
        "norm_final_g": 1.0 + nrm(ks[18], (D_MODEL,), 0.02),
    }


def reference(x, norm_mix_g, w_in, conv_dw_w, conv_dw_b, conv_ln_g, conv_ln_b,
              lambda_q1, lambda_k1, lambda_q2, lambda_k2, subln_g, w_out,
              norm_ffn_g, w_up, ffn_dw_w, ffn_dw_b, w_down, norm_final_g):
    for l in range(DEPTH):
        x = hybrid_layer(x, l, norm_mix_g[l], w_in[l], conv_dw_w[l], conv_dw_b[l],
                         conv_ln_g[l], conv_ln_b[l], lambda_q1[l], lambda_k1[l],
                         lambda_q2[l], lambda_k2[l], subln_g[l], w_out[l],
                         norm_ffn_g[l], w_up[l], ffn_dw_w[l], ffn_dw_b[l], w_down[l])
    return rmsnorm(x, norm_final_g)
```

```python
import functools
import math

import jax
import jax.numpy as jnp
from jax import lax
from jax.experimental import pallas as pl
from jax.experimental.pallas import tpu as pltpu

F32 = jnp.float32
BF16 = jnp.bfloat16

CHUNK = 64
CONV_K = 31
FFN_CONV_K = 3
ATT_HEADS = 8
ATT_QK_DIM = 64
HEAD_W = 2 * ATT_QK_DIM
NORM_EPS = 1e-6
LN_EPS = 1e-5
MASK_VALUE = -1e30

VMEM_LIMIT_BYTES = 56 * 1024 * 1024
SUBLANES = 8
CONV_HALO = 32


def _params(*semantics):
    return pltpu.CompilerParams(dimension_semantics=semantics,
                                vmem_limit_bytes=VMEM_LIMIT_BYTES)


def _resident(block_shape, index_map):
    return pl.BlockSpec(block_shape, index_map, pipeline_mode=pl.Buffered(1))


def _rmsnorm(x, g):
    ms = jnp.mean(x * x, axis=-1, keepdims=True)
    return x * lax.rsqrt(ms + NORM_EPS) * g


def _sigmoid(x):
    return 1.0 / (1.0 + jnp.exp(-x))


def _inproj_kernel(x_ref, g_ref, wa_ref, wg_ref, u_ref, qkv_ref, h_ref):
    j = pl.program_id(1)

    @pl.when(j == 0)
    def _():
        h_ref[...] = _rmsnorm(x_ref[...], g_ref[...]).astype(BF16)
        h = h_ref[...]
        a = jnp.dot(h, wa_ref[...], preferred_element_type=F32)
        gate = jnp.dot(h, wg_ref[...], preferred_element_type=F32)
        u_ref[...] = a * _sigmoid(gate)

    @pl.when(j > 0)
    def _():
        qkv_ref[...] = jnp.dot(h_ref[...], wa_ref[...],
                               preferred_element_type=F32).astype(BF16)


def _inproj(x2d, g, w_in, conv_ch, tm=512):
    t, d = x2d.shape
    n = conv_ch
    n_groups = w_in.shape[1] // n
    assert w_in.shape[1] == n_groups * n and t % tm == 0
    return pl.pallas_call(
        _inproj_kernel,
        grid=(t // tm, n_groups - 1),
        in_specs=[
            pl.BlockSpec((tm, d), lambda i, j: (i, 0)),
            pl.BlockSpec((1, d), lambda i, j: (0, 0)),
            pl.BlockSpec((d, n), lambda i, j: (0, jnp.where(j == 0, 0, j + 1))),
            pl.BlockSpec((d, n), lambda i, j: (0, 1)),
        ],
        out_specs=[
            pl.BlockSpec((tm, n), lambda i, j: (i, 0)),
            pl.BlockSpec((tm, n), lambda i, j: (i, jnp.maximum(j - 1, 0))),
        ],
        out_shape=[
            jax.ShapeDtypeStruct((t, n), F32),
            jax.ShapeDtypeStruct((t, (n_groups - 2) * n), BF16),
        ],
        scratch_shapes=[pltpu.VMEM((tm, d), BF16)],
        compiler_params=_params("parallel", "arbitrary"),
        name="inproj",
    )(x2d, g.reshape(1, d), w_in, w_in)


def _convbranch_kernel(u_ref, halo_ref, w_ref, b_ref, lg_ref, lb_ref, o_ref, win_ref, y_ref,
                       *, row_block):
    t = pl.program_id(1)
    ts, c = u_ref.shape[1], u_ref.shape[2]
    halo = halo_ref[0]
    win_ref[0:CONV_HALO, :] = jnp.where(t > 0, halo, jnp.zeros_like(halo))
    win_ref[CONV_HALO:, :] = u_ref[0]
    first_tap_row = CONV_HALO - (CONV_K - 1)

    for r0 in range(0, ts, row_block):
        for lc in range(c // 128):
            lanes = slice(lc * 128, (lc + 1) * 128)
            acc = jnp.broadcast_to(b_ref[:, lanes], (row_block, 128))
            for k in range(CONV_K):
                acc = acc + win_ref[pl.ds(r0 + first_tap_row + k, row_block), lanes] * w_ref[k:k + 1, lanes]
            y_ref[pl.ds(r0, row_block), lanes] = acc

    y = y_ref[...]
    mu = jnp.mean(y, axis=-1, keepdims=True)
    yc = y - mu
    var = jnp.mean(yc * yc, axis=-1, keepdims=True)
    z = yc * lax.rsqrt(var + LN_EPS) * lg_ref[...] + lb_ref[...]
    o_ref[0] = (z * _sigmoid(z)).astype(BF16)


def _convbranch(u, w, b, ln_g, ln_b, ts=256, row_block=32):
    bsz, seq, c = u.shape
    assert seq % ts == 0 and ts % CONV_HALO == 0 and ts % row_block == 0 and c % 128 == 0
    halo_blocks = ts // CONV_HALO
    return pl.pallas_call(
        functools.partial(_convbranch_kernel, row_block=row_block),
        grid=(bsz, seq // ts),
        in_specs=[
            pl.BlockSpec((1, ts, c), lambda bi, ti: (bi, ti, 0)),
            pl.BlockSpec((1, CONV_HALO, c),
                         lambda bi, ti: (bi, jnp.maximum(ti * halo_blocks - 1, 0), 0)),
            pl.BlockSpec((CONV_K, c), lambda bi, ti: (0, 0)),
            pl.BlockSpec((1, c), lambda bi, ti: (0, 0)),
            pl.BlockSpec((1, c), lambda bi, ti: (0, 0)),
            pl.BlockSpec((1, c), lambda bi, ti: (0, 0)),
        ],
        out_specs=pl.BlockSpec((1, ts, c), lambda bi, ti: (bi, ti, 0)),
        out_shape=jax.ShapeDtypeStruct((bsz, seq, c), BF16),
        scratch_shapes=[pltpu.VMEM((ts + CONV_HALO, c), F32), pltpu.VMEM((ts, c), F32)],
        compiler_params=_params("parallel", "parallel"),
        name="convbranch",
    )(u, u, w, b.reshape(1, c), ln_g.reshape(1, c), ln_b.reshape(1, c))


def _attn_kernel(lamp_ref, q_ref, k_ref, v_ref, sg_ref, o_ref, qz_ref, m_ref, l_ref, acc_ref,
                 *, lambda_init):
    qi = pl.program_id(2)
    tq = q_ref.shape[0]
    tk = tq
    q = q_ref[...] * (1.0 / math.sqrt(ATT_QK_DIM))
    lane = lax.broadcasted_iota(jnp.int32, q.shape, 1)
    zero = jnp.zeros_like(q)
    qz_ref[0:tq, :] = jnp.where(lane < ATT_QK_DIM, q, zero)
    qz_ref[tq:, :] = jnp.where(lane >= ATT_QK_DIM, q, zero)
    m_ref[...] = jnp.full(m_ref.shape, -jnp.inf, F32)
    l_ref[...] = jnp.zeros(l_ref.shape, F32)
    acc_ref[...] = jnp.zeros(acc_ref.shape, F32)

    def block(j, masked):
        k0 = pl.multiple_of(j * tk, tk)
        kb = k_ref[pl.ds(k0, tk), :]
        vb = v_ref[pl.ds(k0, tk), :]
        s = lax.dot_general(qz_ref[...], kb, (((1,), (1,)), ((), ())),
                            preferred_element_type=F32)
        if masked:
            row = lax.broadcasted_iota(jnp.int32, s.shape, 0)
            col = lax.broadcasted_iota(jnp.int32, s.shape, 1)
            q_chunk = jnp.where(row >= tq, row - tq, row) // CHUNK
            s = jnp.where(col // CHUNK <= q_chunk, s, MASK_VALUE)
        m_prev = m_ref[...]
        m_new = jnp.maximum(m_prev, jnp.max(s, axis=-1, keepdims=True))
        alpha = jnp.exp(m_prev - m_new)
        p = jnp.exp(s - m_new)
        l_ref[...] = alpha * l_ref[...] + jnp.sum(p, axis=-1, keepdims=True)
        acc_ref[...] = alpha * acc_ref[...] + jnp.dot(p.astype(BF16), vb,
                                                      preferred_element_type=F32)
        m_ref[...] = m_new

    def body(j, carry):
        block(j, masked=False)
        return carry

    lax.fori_loop(0, qi, body, 0)
    block(qi, masked=True)

    lp = lamp_ref[...]
    s1 = jnp.sum(lp[0:1] * lp[1:2], axis=-1, keepdims=True)
    s2 = jnp.sum(lp[2:3] * lp[3:4], axis=-1, keepdims=True)
    lam = jnp.exp(s1) - jnp.exp(s2) + lambda_init
    o = acc_ref[...] / l_ref[...]
    o = o[0:tq] - lam * o[tq:]
    o_ref[...] = (_rmsnorm(o, sg_ref[...]) * (1.0 - lambda_init)).astype(BF16)


def _attention(qkv, lam_params, subln_g, bsz, seq, lambda_init, tq=256):
    t = qkv.shape[0]
    n_q_tiles = seq // tq
    assert t == bsz * seq and seq % tq == 0 and tq % CHUNK == 0
    k_col0 = ATT_HEADS
    v_col0 = 2 * ATT_HEADS
    return pl.pallas_call(
        functools.partial(_attn_kernel, lambda_init=lambda_init),
        grid=(bsz, ATT_HEADS, n_q_tiles),
        in_specs=[
            pl.BlockSpec((4, ATT_QK_DIM), lambda b, h, qi: (0, 0)),
            pl.BlockSpec((tq, HEAD_W), lambda b, h, qi: (b * n_q_tiles + qi, h)),
            pl.BlockSpec((seq, HEAD_W), lambda b, h, qi: (b, k_col0 + h)),
            pl.BlockSpec((seq, HEAD_W), lambda b, h, qi: (b, v_col0 + h)),
            pl.BlockSpec((1, HEAD_W), lambda b, h, qi: (0, 0)),
        ],
        out_specs=pl.BlockSpec((tq, HEAD_W), lambda b, h, qi: (b * n_q_tiles + qi, h)),
        out_shape=jax.ShapeDtypeStruct((t, ATT_HEADS * HEAD_W), BF16),
        scratch_shapes=[
            pltpu.VMEM((2 * tq, HEAD_W), BF16),
            pltpu.VMEM((2 * tq, 1), F32),
            pltpu.VMEM((2 * tq, 1), F32),
            pltpu.VMEM((2 * tq, HEAD_W), F32),
        ],
        compiler_params=_params("parallel", "parallel", "arbitrary"),
        name="diffattn",
    )(lam_params, qkv, qkv, qkv, subln_g.reshape(1, HEAD_W))


def _outproj_kernel(c_ref, a_ref, x_ref, wc_ref, wa_ref, g_ref, x1_ref, h_ref):
    mix = jnp.dot(c_ref[...], wc_ref[...], preferred_element_type=F32)
    mix = mix + jnp.dot(a_ref[...], wa_ref[...], preferred_element_type=F32)
    x1 = x_ref[...] + mix
    x1_ref[...] = x1
    h_ref[...] = _rmsnorm(x1, g_ref[...]).astype(BF16)


def _outproj(conv_out, attn_out, x2d, w_out, g, tm=256):
    t, d = x2d.shape
    c = conv_out.shape[1]
    a = attn_out.shape[1]
    assert w_out.shape == (c + a, d) and c == a and t % tm == 0
    return pl.pallas_call(
        _outproj_kernel,
        grid=(t // tm,),
        in_specs=[
            pl.BlockSpec((tm, c), lambda i: (i, 0)),
            pl.BlockSpec((tm, a), lambda i: (i, 0)),
            pl.BlockSpec((tm, d), lambda i: (i, 0)),
            _resident((c, d), lambda i: (0, 0)),
            _resident((a, d), lambda i: (1, 0)),
            pl.BlockSpec((1, d), lambda i: (0, 0)),
        ],
        out_specs=[pl.BlockSpec((tm, d), lambda i: (i, 0)),
                   pl.BlockSpec((tm, d), lambda i: (i, 0))],
        out_shape=[jax.ShapeDtypeStruct((t, d), F32), jax.ShapeDtypeStruct((t, d), BF16)],
        compiler_params=_params("parallel"),
        name="outproj",
    )(conv_out, attn_out, x2d, w_out, w_out, g.reshape(1, d))


def _ffnup_kernel(h_ref, wg_ref, wu_ref, cw_ref, cb_ref, o_ref, gbuf_ref, *, tiles_per_seq):
    i = pl.program_id(1)
    tm = h_ref.shape[0]
    h = h_ref[...]
    gate = jnp.dot(h, wg_ref[...], preferred_element_type=F32)
    up = jnp.dot(h, wu_ref[...], preferred_element_type=F32)

    @pl.when(i % tiles_per_seq == 0)
    def _():
        gbuf_ref[0:SUBLANES, :] = jnp.zeros((SUBLANES, gbuf_ref.shape[1]), F32)

    gbuf_ref[SUBLANES:, :] = gate
    conv = cb_ref[...] + cw_ref[2:3, :] * gate
    conv = conv + cw_ref[1:2, :] * gbuf_ref[pl.ds(SUBLANES - 1, tm), :]
    conv = conv + cw_ref[0:1, :] * gbuf_ref[pl.ds(SUBLANES - 2, tm), :]
    gelu = 0.5 * conv * (1.0 + lax.erf(conv * math.sqrt(0.5)))
    o_ref[...] = (gelu * up).astype(BF16)
    gbuf_ref[0:SUBLANES, :] = gbuf_ref[tm:tm + SUBLANES, :]


def _ffnup(h2, w_up, conv_w, conv_b, seq, tm=1024, tn=512):
    t, d = h2.shape
    d_ff = w_up.shape[1] // 2
    assert t % tm == 0 and seq % tm == 0 and d_ff % tn == 0 and conv_w.shape[0] == FFN_CONV_K
    n_col = d_ff // tn
    return pl.pallas_call(
        functools.partial(_ffnup_kernel, tiles_per_seq=seq // tm),
        grid=(n_col, t // tm),
        in_specs=[
            pl.BlockSpec((tm, d), lambda j, i: (i, 0)),
            pl.BlockSpec((d, tn), lambda j, i: (0, j)),
            pl.BlockSpec((d, tn), lambda j, i: (0, n_col + j)),
            pl.BlockSpec((FFN_CONV_K, tn), lambda j, i: (0, j)),
            pl.BlockSpec((1, tn), lambda j, i: (0, j)),
        ],
        out_specs=pl.BlockSpec((tm, tn), lambda j, i: (i, j)),
        out_shape=jax.ShapeDtypeStruct((t, d_ff), BF16),
        scratch_shapes=[pltpu.VMEM((tm + SUBLANES, tn), F32)],
        compiler_params=_params("parallel", "arbitrary"),
        name="ffnup",
    )(h2, w_up, w_up, conv_w, conv_b.reshape(1, d_ff))


def _ffndown_kernel(a_ref, w_ref, x1_ref, g_ref, o_ref, *, final_norm):
    x2 = x1_ref[...] + jnp.dot(a_ref[...], w_ref[...], preferred_element_type=F32)
    o_ref[...] = _rmsnorm(x2, g_ref[...]) if final_norm else x2


def _ffndown(act, w_down, x1, g, final_norm, tm=256):
    t, d_ff = act.shape
    d = w_down.shape[1]
    assert t % tm == 0
    return pl.pallas_call(
        functools.partial(_ffndown_kernel, final_norm=final_norm),
        grid=(t // tm,),
        in_specs=[
            pl.BlockSpec((tm, d_ff), lambda i: (i, 0)),
            _resident((d_ff, d), lambda i: (0, 0)),
            pl.BlockSpec((tm, d), lambda i: (i, 0)),
            pl.BlockSpec((1, d), lambda i: (0, 0)),
        ],
        out_specs=pl.BlockSpec((tm, d), lambda i: (i, 0)),
        out_shape=jax.ShapeDtypeStruct((t, d), F32),
        compiler_params=_params("parallel"),
        name="ffndown",
    )(act, w_down, x1, g.reshape(1, d))


def _layer(x2d, bsz, seq, layer_idx, is_last, norm_final_g, norm_mix_g, w_in, conv_dw_w,
           conv_dw_b, conv_ln_g, conv_ln_b, lambda_q1, lambda_k1, lambda_q2, lambda_k2, subln_g,
           w_out, norm_ffn_g, w_up, ffn_dw_w, ffn_dw_b, w_down):
    conv_ch = conv_dw_w.shape[1]
    lambda_init = 0.8 - 0.6 * math.exp(-0.3 * layer_idx)
    u, qkv = _inproj(x2d, norm_mix_g, w_in.astype(BF16), conv_ch)
    conv_out = _convbranch(u.reshape(bsz, seq, conv_ch), conv_dw_w, conv_dw_b, conv_ln_g,
                           conv_ln_b).reshape(bsz * seq, conv_ch)
    lam_params = jnp.stack([lambda_q1, lambda_k1, lambda_q2, lambda_k2]).astype(F32)
    attn_out = _attention(qkv, lam_params, subln_g, bsz, seq, lambda_init)
    x1, h2 = _outproj(conv_out, attn_out, x2d, w_out.astype(BF16), norm_ffn_g)
    act = _ffnup(h2, w_up.astype(BF16), ffn_dw_w, ffn_dw_b, seq)
    final_g = norm_final_g if is_last else jnp.ones_like(norm_final_g)
    return _ffndown(act, w_down.astype(BF16), x1, final_g, final_norm=is_last)


def kernel(x, norm_mix_g, w_in, conv_dw_w, conv_dw_b, conv_ln_g, conv_ln_b, lambda_q1, lambda_k1,
           lambda_q2, lambda_k2, subln_g, w_out, norm_ffn_g, w_up, ffn_dw_w, ffn_dw_b, w_down,
           norm_final_g):
    bsz, seq, d = x.shape
    depth = w_in.shape[0]
    x2d = x.reshape(bsz * seq, d)
    for l in range(depth):
        x2d = _layer(x2d, bsz, seq, l, l == depth - 1, norm_final_g, norm_mix_g[l], w_in[l],
                     conv_dw_w[l], conv_dw_b[l], conv_ln_g[l], conv_ln_b[l], lambda_q1[l],
                     lambda_k1[l], lambda_q2[l], lambda_k2[l], subln_g[l], w_out[l],
                     norm_ffn_g[l], w_up[l], ffn_dw_w[l], ffn_dw_b[l], w_down[l])
    return x2d.reshape(bsz, seq, d)
```

```python
import functools
import math

import jax
import jax.numpy as jnp
from jax import lax
from jax.experimental import pallas as pl
from jax.experimental.pallas import tpu as pltpu

F32 = jnp.float32
BF16 = jnp.bfloat16

CHUNK = 64
CONV_K = 31
FFN_CONV_K = 3
ATT_HEADS = 8
ATT_QK_DIM = 64
HEAD_W = 2 * ATT_QK_DIM
NORM_EPS = 1e-6
LN_EPS = 1e-5
MASK_VALUE = -1e30

VMEM_LIMIT_BYTES = 56 * 1024 * 1024
SUBLANES = 8
CONV_HALO = 32


def _params(*semantics):
    return pltpu.CompilerParams(dimension_semantics=semantics,
                                vmem_limit_bytes=VMEM_LIMIT_BYTES)


def _resident(block_shape, index_map):
    return pl.BlockSpec(block_shape, index_map, pipeline_mode=pl.Buffered(1))


def _rmsnorm(x, g):
    ms = jnp.mean(x * x, axis=-1, keepdims=True)
    return x * lax.rsqrt(ms + NORM_EPS) * g


def _sigmoid(x):
    return 1.0 / (1.0 + jnp.exp(-x))


def _inproj_kernel(x_ref, g_ref, wa_ref, wg_ref, u_ref, qkv_ref, h_ref):
    j = pl.program_id(1)

    @pl.when(j == 0)
    def _():
        h_ref[...] = _rmsnorm(x_ref[...], g_ref[...]).astype(BF16)
        h = h_ref[...]
        a = jnp.dot(h, wa_ref[...], preferred_element_type=F32)
        gate = jnp.dot(h, wg_ref[...], preferred_element_type=F32)
        u_ref[...] = a * _sigmoid(gate)

    @pl.when(j > 0)
    def _():
        qkv_ref[...] = jnp.dot(h_ref[...], wa_ref[...],
                               preferred_element_type=F32).astype(BF16)


def _inproj(x2d, g, w_in, conv_ch, tm=512):
    t, d = x2d.shape
    n = conv_ch
    n_groups = w_in.shape[1] // n
    assert w_in.shape[1] == n_groups * n and t % tm == 0
    return pl.pallas_call(
        _inproj_kernel,
        grid=(t // tm, n_groups - 1),
        in_specs=[
            pl.BlockSpec((tm, d), lambda i, j: (i, 0)),
            pl.BlockSpec((1, d), lambda i, j: (0, 0)),
            pl.BlockSpec((d, n), lambda i, j: (0, jnp.where(j == 0, 0, j + 1))),
            pl.BlockSpec((d, n), lambda i, j: (0, 1)),
        ],
        out_specs=[
            pl.BlockSpec((tm, n), lambda i, j: (i, 0)),
            pl.BlockSpec((tm, n), lambda i, j: (i, jnp.maximum(j - 1, 0))),
        ],
        out_shape=[
            jax.ShapeDtypeStruct((t, n), F32),
            jax.ShapeDtypeStruct((t, (n_groups - 2) * n), BF16),
        ],
        scratch_shapes=[pltpu.VMEM((tm, d), BF16)],
        compiler_params=_params("parallel", "arbitrary"),
        name="inproj",
    )(x2d, g.reshape(1, d), w_in, w_in)


def _convbranch_kernel(u_ref, halo_ref, w_ref, b_ref, lg_ref, lb_ref, o_ref, win_ref, y_ref,
                       *, row_block):
    t = pl.program_id(1)
    ts, c = u_ref.shape[1], u_ref.shape[2]
    halo = halo_ref[0]
    win_ref[0:CONV_HALO, :] = jnp.where(t > 0, halo, jnp.zeros_like(halo))
    win_ref[CONV_HALO:, :] = u_ref[0]
    first_tap_row = CONV_HALO - (CONV_K - 1)

    for r0 in range(0, ts, row_block):
        for lc in range(c // 128):
            lanes = slice(lc * 128, (lc + 1) * 128)
            acc = jnp.broadcast_to(b_ref[:, lanes], (row_block, 128))
            for k in range(CONV_K):
                acc = acc + win_ref[pl.ds(r0 + first_tap_row + k, row_block), lanes] * w_ref[k:k + 1, lanes]
            y_ref[pl.ds(r0, row_block), lanes] = acc

    y = y_ref[...]
    mu = jnp.mean(y, axis=-1, keepdims=True)
    yc = y - mu
    var = jnp.mean(yc * yc, axis=-1, keepdims=True)
    z = yc * lax.rsqrt(var + LN_EPS) * lg_ref[...] + lb_ref[...]
    o_ref[0] = (z * _sigmoid(z)).astype(BF16)


def _convbranch(u, w, b, ln_g, ln_b, ts=256, row_block=32):
    bsz, seq, c = u.shape
    assert seq % ts == 0 and ts % CONV_HALO == 0 and ts % row_block == 0 and c % 128 == 0
    halo_blocks = ts // CONV_HALO
    return pl.pallas_call(
        functools.partial(_convbranch_kernel, row_block=row_block),
        grid=(bsz, seq // ts),
        in_specs=[
            pl.BlockSpec((1, ts, c), lambda bi, ti: (bi, ti, 0)),
            pl.BlockSpec((1, CONV_HALO, c),
                         lambda bi, ti: (bi, jnp.maximum(ti * halo_blocks - 1, 0), 0)),
            pl.BlockSpec((CONV_K, c), lambda bi, ti: (0, 0)),
            pl.BlockSpec((1, c), lambda bi, ti: (0, 0)),
            pl.BlockSpec((1, c), lambda bi, ti: (0, 0)),
            pl.BlockSpec((1, c), lambda bi, ti: (0, 0)),
        ],
        out_specs=pl.BlockSpec((1, ts, c), lambda bi, ti: (bi, ti, 0)),
        out_shape=jax.ShapeDtypeStruct((bsz, seq, c), BF16),
        scratch_shapes=[pltpu.VMEM((ts + CONV_HALO, c), F32), pltpu.VMEM((ts, c), F32)],
        compiler_params=_params("parallel", "parallel"),
        name="convbranch",
    )(u, u, w, b.reshape(1, c), ln_g.reshape(1, c), ln_b.reshape(1, c))


def _attn_kernel(lamp_ref, q_ref, k_ref, v_ref, sg_ref, o_ref, qz_ref, vt_ref, st_ref, m_ref,
                 l_ref, acc_ref, *, lambda_init):
    qi = pl.program_id(2)
    tq = q_ref.shape[0]
    n_heads, n_blocks, _, tk = vt_ref.shape
    assert tk == tq

    def head_lanes(g):
        return slice(g * HEAD_W, (g + 1) * HEAD_W)

    @pl.when(qi == 0)
    def _():
        for g in range(n_heads):
            for j in range(n_blocks):
                vt_ref[g, j] = v_ref[j * tk:(j + 1) * tk, head_lanes(g)].astype(F32).T.astype(BF16)

    for g in range(n_heads):
        q = q_ref[:, head_lanes(g)] * (1.0 / math.sqrt(ATT_QK_DIM))
        lane = lax.broadcasted_iota(jnp.int32, q.shape, 1)
        zero = jnp.zeros_like(q)
        qz_ref[g, 0:tq, :] = jnp.where(lane < ATT_QK_DIM, q, zero)
        qz_ref[g, tq:, :] = jnp.where(lane >= ATT_QK_DIM, q, zero)
    m_ref[...] = jnp.full(m_ref.shape, -jnp.inf, F32)
    l_ref[...] = jnp.zeros(l_ref.shape, F32)
    acc_ref[...] = jnp.zeros(acc_ref.shape, F32)

    def scores(j, g, diagonal):
        k0 = pl.multiple_of(j * tk, tk)
        kb = k_ref[pl.ds(k0, tk), head_lanes(g)]
        st = lax.dot_general(kb, qz_ref[g], (((1,), (1,)), ((), ())),
                             preferred_element_type=F32)
        if diagonal is not False:
            key = lax.broadcasted_iota(jnp.int32, st.shape, 0)
            col = lax.broadcasted_iota(jnp.int32, st.shape, 1)
            q_chunk = jnp.where(col >= tq, col - tq, col) // CHUNK
            allowed = key // CHUNK <= q_chunk
            if diagonal is None:
                allowed = jnp.logical_or(allowed, j != qi)
            st = jnp.where(allowed, st, MASK_VALUE)
        return st

    def softmax_pv(j, g, st):
        m_prev = m_ref[g]
        m_new = jnp.maximum(m_prev, jnp.max(st, axis=0, keepdims=True))
        alpha = jnp.exp(m_prev - m_new)
        p = jnp.exp(st - m_new)
        l_ref[g] = alpha * l_ref[g] + jnp.sum(p, axis=0, keepdims=True)
        acc_ref[g] = alpha * acc_ref[g] + jnp.dot(vt_ref[g, j], p.astype(BF16),
                                                  preferred_element_type=F32)
        m_ref[g] = m_new

    def key_block(j, diagonal):
        st = st_ref[...]
        st_next = None
        for g in range(n_heads):
            if g + 1 < n_heads:
                st_next = scores(j, g + 1, diagonal)
            elif not diagonal:
                st_ref[...] = scores(j + 1, 0, None)
            softmax_pv(j, g, st)
            st = st_next

    def body(j, carry):
        key_block(j, diagonal=False)
        return carry

    st_ref[...] = scores(0, 0, None)
    lax.fori_loop(0, qi, body, 0)
    key_block(qi, diagonal=True)

    lp = lamp_ref[...]
    s1 = jnp.sum(lp[0:1] * lp[1:2], axis=-1, keepdims=True)
    s2 = jnp.sum(lp[2:3] * lp[3:4], axis=-1, keepdims=True)
    lam = jnp.exp(s1) - jnp.exp(s2) + lambda_init
    for g in range(n_heads):
        o = acc_ref[g] / l_ref[g]
        o = o[:, 0:tq] - lam * o[:, tq:]
        ms = jnp.mean(o * o, axis=0, keepdims=True)
        y = (o * lax.rsqrt(ms + NORM_EPS)).T
        o_ref[:, head_lanes(g)] = (y * sg_ref[...] * (1.0 - lambda_init)).astype(BF16)


def _attention(qkv, lam_params, subln_g, bsz, seq, lambda_init, tq=256, heads_per_step=8):
    t = qkv.shape[0]
    n_q_tiles = seq // tq
    gw = heads_per_step * HEAD_W
    n_groups = ATT_HEADS // heads_per_step
    assert t == bsz * seq and seq % tq == 0 and tq % CHUNK == 0 and ATT_HEADS % heads_per_step == 0
    return pl.pallas_call(
        functools.partial(_attn_kernel, lambda_init=lambda_init),
        grid=(bsz, n_groups, n_q_tiles),
        in_specs=[
            pl.BlockSpec((4, ATT_QK_DIM), lambda b, hg, qi: (0, 0)),
            pl.BlockSpec((tq, gw), lambda b, hg, qi: (b * n_q_tiles + qi, hg)),
            pl.BlockSpec((seq, gw), lambda b, hg, qi: (b, n_groups + hg)),
            pl.BlockSpec((seq, gw), lambda b, hg, qi: (b, 2 * n_groups + hg)),
            pl.BlockSpec((1, HEAD_W), lambda b, hg, qi: (0, 0)),
        ],
        out_specs=pl.BlockSpec((tq, gw), lambda b, hg, qi: (b * n_q_tiles + qi, hg)),
        out_shape=jax.ShapeDtypeStruct((t, ATT_HEADS * HEAD_W), BF16),
        scratch_shapes=[
            pltpu.VMEM((heads_per_step, 2 * tq, HEAD_W), BF16),
            pltpu.VMEM((heads_per_step, seq // tq, HEAD_W, tq), BF16),
            pltpu.VMEM((tq, 2 * tq), F32),
            pltpu.VMEM((heads_per_step, 1, 2 * tq), F32),
            pltpu.VMEM((heads_per_step, 1, 2 * tq), F32),
            pltpu.VMEM((heads_per_step, HEAD_W, 2 * tq), F32),
        ],
        compiler_params=_params("parallel", "parallel", "arbitrary"),
        name="diffattn",
    )(lam_params, qkv, qkv, qkv, subln_g.reshape(1, HEAD_W))


def _outproj_kernel(c_ref, a_ref, x_ref, wc_ref, wa_ref, g_ref, x1_ref, h_ref):
    mix = jnp.dot(c_ref[...], wc_ref[...], preferred_element_type=F32)
    mix = mix + jnp.dot(a_ref[...], wa_ref[...], preferred_element_type=F32)
    x1 = x_ref[...] + mix
    x1_ref[...] = x1
    h_ref[...] = _rmsnorm(x1, g_ref[...]).astype(BF16)


def _outproj(conv_out, attn_out, x2d, w_out, g, tm=256):
    t, d = x2d.shape
    c = conv_out.shape[1]
    a = attn_out.shape[1]
    assert w_out.shape == (c + a, d) and c == a and t % tm == 0
    return pl.pallas_call(
        _outproj_kernel,
        grid=(t // tm,),
        in_specs=[
            pl.BlockSpec((tm, c), lambda i: (i, 0)),
            pl.BlockSpec((tm, a), lambda i: (i, 0)),
            pl.BlockSpec((tm, d), lambda i: (i, 0)),
            _resident((c, d), lambda i: (0, 0)),
            _resident((a, d), lambda i: (1, 0)),
            pl.BlockSpec((1, d), lambda i: (0, 0)),
        ],
        out_specs=[pl.BlockSpec((tm, d), lambda i: (i, 0)),
                   pl.BlockSpec((tm, d), lambda i: (i, 0))],
        out_shape=[jax.ShapeDtypeStruct((t, d), F32), jax.ShapeDtypeStruct((t, d), BF16)],
        compiler_params=_params("parallel"),
        name="outproj",
    )(conv_out, attn_out, x2d, w_out, w_out, g.reshape(1, d))


def _ffnup_kernel(h_ref, wg_ref, wu_ref, cw_ref, cb_ref, o_ref, gbuf_ref, *, tiles_per_seq):
    i = pl.program_id(1)
    tm = h_ref.shape[0]
    h = h_ref[...]
    gate = jnp.dot(h, wg_ref[...], preferred_element_type=F32)
    up = jnp.dot(h, wu_ref[...], preferred_element_type=F32)

    @pl.when(i % tiles_per_seq == 0)
    def _():
        gbuf_ref[0:SUBLANES, :] = jnp.zeros((SUBLANES, gbuf_ref.shape[1]), F32)

    gbuf_ref[SUBLANES:, :] = gate
    conv = cb_ref[...] + cw_ref[2:3, :] * gate
    conv = conv + cw_ref[1:2, :] * gbuf_ref[pl.ds(SUBLANES - 1, tm), :]
    conv = conv + cw_ref[0:1, :] * gbuf_ref[pl.ds(SUBLANES - 2, tm), :]
    gelu = 0.5 * conv * (1.0 + lax.erf(conv * math.sqrt(0.5)))
    o_ref[...] = (gelu * up).astype(BF16)
    gbuf_ref[0:SUBLANES, :] = gbuf_ref[tm:tm + SUBLANES, :]


def _ffnup(h2, w_up, conv_w, conv_b, seq, tm=1024, tn=512):
    t, d = h2.shape
    d_ff = w_up.shape[1] // 2
    assert t % tm == 0 and seq % tm == 0 and d_ff % tn == 0 and conv_w.shape[0] == FFN_CONV_K
    n_col = d_ff // tn
    return pl.pallas_call(
        functools.partial(_ffnup_kernel, tiles_per_seq=seq // tm),
        grid=(n_col, t // tm),
        in_specs=[
            pl.BlockSpec((tm, d), lambda j, i: (i, 0)),
            pl.BlockSpec((d, tn), lambda j, i: (0, j)),
            pl.BlockSpec((d, tn), lambda j, i: (0, n_col + j)),
            pl.BlockSpec((FFN_CONV_K, tn), lambda j, i: (0, j)),
            pl.BlockSpec((1, tn), lambda j, i: (0, j)),
        ],
        out_specs=pl.BlockSpec((tm, tn), lambda j, i: (i, j)),
        out_shape=jax.ShapeDtypeStruct((t, d_ff), BF16),
        scratch_shapes=[pltpu.VMEM((tm + SUBLANES, tn), F32)],
        compiler_params=_params("parallel", "arbitrary"),
        name="ffnup",
    )(h2, w_up, w_up, conv_w, conv_b.reshape(1, d_ff))


def _ffndown_kernel(a_ref, w_ref, x1_ref, g_ref, o_ref, *, final_norm):
    x2 = x1_ref[...] + jnp.dot(a_ref[...], w_ref[...], preferred_element_type=F32)
    o_ref[...] = _rmsnorm(x2, g_ref[...]) if final_norm else x2


def _ffndown(act, w_down, x1, g, final_norm, tm=256):
    t, d_ff = act.shape
    d = w_down.shape[1]
    assert t % tm == 0
    return pl.pallas_call(
        functools.partial(_ffndown_kernel, final_norm=final_norm),
        grid=(t // tm,),
        in_specs=[
            pl.BlockSpec((tm, d_ff), lambda i: (i, 0)),
            _resident((d_ff, d), lambda i: (0, 0)),
            pl.BlockSpec((tm, d), lambda i: (i, 0)),
            pl.BlockSpec((1, d), lambda i: (0, 0)),
        ],
        out_specs=pl.BlockSpec((tm, d), lambda i: (i, 0)),
        out_shape=jax.ShapeDtypeStruct((t, d), F32),
        compiler_params=_params("parallel"),
        name="ffndown",
    )(act, w_down, x1, g.reshape(1, d))


def _layer(x2d, bsz, seq, layer_idx, is_last, norm_final_g, norm_mix_g, w_in, conv_dw_w,
           conv_dw_b, conv_ln_g, conv_ln_b, lambda_q1, lambda_k1, lambda_q2, lambda_k2, subln_g,
           w_out, norm_ffn_g, w_up, ffn_dw_w, ffn_dw_b, w_down):
    conv_ch = conv_dw_w.shape[1]
    lambda_init = 0.8 - 0.6 * math.exp(-0.3 * layer_idx)
    u, qkv = _inproj(x2d, norm_mix_g, w_in.astype(BF16), conv_ch)
    conv_out = _convbranch(u.reshape(bsz, seq, conv_ch), conv_dw_w, conv_dw_b, conv_ln_g,
                           conv_ln_b).reshape(bsz * seq, conv_ch)
    lam_params = jnp.stack([lambda_q1, lambda_k1, lambda_q2, lambda_k2]).astype(F32)
    attn_out = _attention(qkv, lam_params, subln_g, bsz, seq, lambda_init)
    x1, h2 = _outproj(conv_out, attn_out, x2d, w_out.astype(BF16), norm_ffn_g)
    act = _ffnup(h2, w_up.astype(BF16), ffn_dw_w, ffn_dw_b, seq)
    final_g = norm_final_g if is_last else jnp.ones_like(norm_final_g)
    return _ffndown(act, w_down.astype(BF16), x1, final_g, final_norm=is_last)


def kernel(x, norm_mix_g, w_in, conv_dw_w, conv_dw_b, conv_ln_g, conv_ln_b, lambda_q1, lambda_k1,
           lambda_q2, lambda_k2, subln_g, w_out, norm_ffn_g, w_up, ffn_dw_w, ffn_dw_b, w_down,
           norm_final_g):
    bsz, seq, d = x.shape
    depth = w_in.shape[0]
    x2d = x.reshape(bsz * seq, d)
    for l in range(depth):
        x2d = _layer(x2d, bsz, seq, l, l == depth - 1, norm_final_g, norm_mix_g[l], w_in[l],
                     conv_dw_w[l], conv_dw_b[l], conv_ln_g[l], conv_ln_b[l], lambda_q1[l],
                     lambda_k1[l], lambda_q2[l], lambda_k2[l], subln_g[l], w_out[l],
                     norm_ffn_g[l], w_up[l], ffn_dw_w[l], ffn_dw_b[l], w_down[l])
    return x2d.reshape(bsz, seq, d)
```

```python
import functools
import math

import jax
import jax.numpy as jnp
from jax import lax
from jax.experimental import pallas as pl
from jax.experimental.pallas import tpu as pltpu

F32 = jnp.float32
BF16 = jnp.bfloat16

CHUNK = 64
CONV_K = 31
FFN_CONV_K = 3
ATT_HEADS = 8
ATT_QK_DIM = 64
HEAD_W = 2 * ATT_QK_DIM
NORM_EPS = 1e-6
LN_EPS = 1e-5
MASK_VALUE = -1e30
Q_PRESCALE = math.log2(math.e) / math.sqrt(ATT_QK_DIM)

VMEM_LIMIT_BYTES = 56 * 1024 * 1024
SUBLANES = 8
LANES = 128
BF16_SUBLANES = 16
CONV_HALO = 32


def _params(*semantics):
    return pltpu.CompilerParams(dimension_semantics=semantics,
                                vmem_limit_bytes=VMEM_LIMIT_BYTES)


def _resident(block_shape, index_map):
    return pl.BlockSpec(block_shape, index_map, pipeline_mode=pl.Buffered(1))


def _rmsnorm(x, g):
    ms = jnp.mean(x * x, axis=-1, keepdims=True)
    return x * lax.rsqrt(ms + NORM_EPS) * g


def _sigmoid(x):
    return 1.0 / (1.0 + jnp.exp(-x))


def _inproj_kernel(x_ref, g_ref, w_ref, u_ref, qkv_ref, *, row_chunk):
    tm, n = u_ref.shape
    n_qkv = qkv_ref.shape[1] // n
    for r0 in range(0, tm, row_chunk):
        rows = slice(r0, r0 + row_chunk)
        h = _rmsnorm(x_ref[rows, :], g_ref[...]).astype(BF16)
        a = jnp.dot(h, w_ref[:, 0:n], preferred_element_type=F32)
        gate = jnp.dot(h, w_ref[:, n:2 * n], preferred_element_type=F32)
        u_ref[rows, :] = a * _sigmoid(gate)
        for c in range(n_qkv):
            z = jnp.dot(h, w_ref[:, (2 + c) * n:(3 + c) * n], preferred_element_type=F32)
            if c == 0:
                z = z * Q_PRESCALE
            qkv_ref[rows, c * n:(c + 1) * n] = z.astype(BF16)


def _inproj(x2d, g, w_in, conv_ch, tm=512, row_chunk=256):
    t, d = x2d.shape
    n = conv_ch
    n_groups = w_in.shape[1] // n
    assert w_in.shape[1] == n_groups * n and t % tm == 0 and tm % row_chunk == 0
    return pl.pallas_call(
        functools.partial(_inproj_kernel, row_chunk=row_chunk),
        grid=(t // tm,),
        in_specs=[
            pl.BlockSpec((tm, d), lambda i: (i, 0)),
            pl.BlockSpec((1, d), lambda i: (0, 0)),
            _resident((d, n_groups * n), lambda i: (0, 0)),
        ],
        out_specs=[
            pl.BlockSpec((tm, n), lambda i: (i, 0)),
            pl.BlockSpec((tm, (n_groups - 2) * n), lambda i: (i, 0)),
        ],
        out_shape=[
            jax.ShapeDtypeStruct((t, n), F32),
            jax.ShapeDtypeStruct((t, (n_groups - 2) * n), BF16),
        ],
        compiler_params=_params("parallel"),
        name="inproj",
    )(x2d, g.reshape(1, d), w_in)


def _convbranch_kernel(u_ref, halo_ref, w_ref, b_ref, lg_ref, lb_ref, o_ref, win_ref, y_ref):
    t = pl.program_id(1)
    ts, c = u_ref.shape[1], u_ref.shape[2]
    n_chunks = c // LANES
    n_groups = ts // SUBLANES
    first_tap_row = CONV_HALO - (CONV_K - 1)

    for lc in range(n_chunks):
        lanes = slice(lc * LANES, (lc + 1) * LANES)
        halo = halo_ref[0, :, lanes]
        win_ref[lc, 0:CONV_HALO, :] = jnp.where(t > 0, halo, jnp.zeros_like(halo))
        win_ref[lc, CONV_HALO:, :] = u_ref[0, :, lanes]

    def conv_chunk(lc, carry):
        bias = jnp.broadcast_to(b_ref[lc], (SUBLANES, LANES))
        accs = [bias for _ in range(n_groups)]
        for b in range(SUBLANES):
            taps = list(range(b, CONV_K, SUBLANES))
            wks = [jnp.broadcast_to(w_ref[lc, k:k + 1, :], (SUBLANES, LANES)) for k in taps]
            for j in range(n_groups + len(taps) - 1):
                xw = win_ref[lc, pl.ds(first_tap_row + b + SUBLANES * j, SUBLANES), :]
                for a, wk in enumerate(wks):
                    g = j - a
                    if 0 <= g < n_groups:
                        accs[g] = accs[g] + xw * wk
        for g in range(n_groups):
            y_ref[lc, pl.ds(SUBLANES * g, SUBLANES), :] = accs[g]
        return carry

    lax.fori_loop(0, n_chunks, conv_chunk, 0)

    y = y_ref[...]
    mu = jnp.sum(jnp.sum(y, axis=0), axis=-1, keepdims=True) * (1.0 / c)
    yc = y - mu
    var = jnp.sum(jnp.sum(yc * yc, axis=0), axis=-1, keepdims=True) * (1.0 / c)
    z = yc * lax.rsqrt(var + LN_EPS) * lg_ref[...] + lb_ref[...]
    act = (z * _sigmoid(z)).astype(BF16)
    for lc in range(n_chunks):
        o_ref[0, :, lc * LANES:(lc + 1) * LANES] = act[lc]


def _convbranch(u, w, b, ln_g, ln_b, ts=256):
    bsz, seq, c = u.shape
    assert seq % ts == 0 and ts % CONV_HALO == 0 and c % LANES == 0 and w.shape[0] == CONV_K
    halo_blocks = ts // CONV_HALO
    n_chunks = c // LANES

    def per_chunk(p):
        return p.reshape(-1, n_chunks, LANES).transpose(1, 0, 2)

    def whole(rows):
        return pl.BlockSpec((n_chunks, rows, LANES), lambda bi, ti: (0, 0, 0))

    return pl.pallas_call(
        _convbranch_kernel,
        grid=(bsz, seq // ts),
        in_specs=[
            pl.BlockSpec((1, ts, c), lambda bi, ti: (bi, ti, 0)),
            pl.BlockSpec((1, CONV_HALO, c),
                         lambda bi, ti: (bi, jnp.maximum(ti * halo_blocks - 1, 0), 0)),
            whole(CONV_K), whole(1), whole(1), whole(1),
        ],
        out_specs=pl.BlockSpec((1, ts, c), lambda bi, ti: (bi, ti, 0)),
        out_shape=jax.ShapeDtypeStruct((bsz, seq, c), BF16),
        scratch_shapes=[pltpu.VMEM((n_chunks, ts + CONV_HALO, LANES), F32),
                        pltpu.VMEM((n_chunks, ts, LANES), F32)],
        compiler_params=_params("parallel", "parallel"),
        name="convbranch",
    )(u, u, per_chunk(w), per_chunk(b), per_chunk(ln_g), per_chunk(ln_b))


def _attn_kernel(lamp_ref, q_ref, k_ref, v_ref, sg_ref, o_ref, qz_ref, vt_ref, st_ref, m_ref,
                 acc_ref, *, lambda_init):
    qi = pl.program_id(2)
    tq = q_ref.shape[0]
    n_heads, n_blocks, vt_rows, tk = vt_ref.shape
    assert tk == tq

    def head_lanes(g):
        return slice(g * HEAD_W, (g + 1) * HEAD_W)

    @pl.when(qi == 0)
    def _():
        pad_row = lax.broadcasted_iota(jnp.int32, (vt_rows - HEAD_W, tk), 0)
        ones_then_zeros = jnp.where(pad_row == 0, 1.0, 0.0).astype(BF16)
        for g in range(n_heads):
            for j in range(n_blocks):
                vt_ref[g, j, 0:HEAD_W, :] = (
                    v_ref[j * tk:(j + 1) * tk, head_lanes(g)].astype(F32).T.astype(BF16))
                vt_ref[g, j, HEAD_W:, :] = ones_then_zeros

    for g in range(n_heads):
        q = q_ref[:, head_lanes(g)]
        lane = lax.broadcasted_iota(jnp.int32, q.shape, 1)
        zero = jnp.zeros_like(q)
        qz_ref[g, 0:tq, :] = jnp.where(lane < ATT_QK_DIM, q, zero)
        qz_ref[g, tq:, :] = jnp.where(lane >= ATT_QK_DIM, q, zero)
    m_ref[...] = jnp.full(m_ref.shape, -jnp.inf, F32)
    acc_ref[...] = jnp.zeros(acc_ref.shape, F32)

    def scores(j, g, diagonal):
        k0 = pl.multiple_of(j * tk, tk)
        kb = k_ref[pl.ds(k0, tk), head_lanes(g)]
        st = lax.dot_general(kb, qz_ref[g], (((1,), (1,)), ((), ())),
                             preferred_element_type=F32)
        if diagonal is not False:
            key = lax.broadcasted_iota(jnp.int32, st.shape, 0)
            col = lax.broadcasted_iota(jnp.int32, st.shape, 1)
            q_chunk = jnp.where(col >= tq, col - tq, col) // CHUNK
            allowed = key // CHUNK <= q_chunk
            if diagonal is None:
                allowed = jnp.logical_or(allowed, j != qi)
            st = jnp.where(allowed, st, MASK_VALUE)
        return st

    def softmax_pv(j, g, st):
        m_prev = m_ref[g]
        m_new = jnp.maximum(m_prev, jnp.max(st, axis=0, keepdims=True))
        alpha = jnp.exp2(m_prev - m_new)
        p = jnp.exp2(st - m_new)
        acc_ref[g] = alpha * acc_ref[g] + jnp.dot(vt_ref[g, j], p.astype(BF16),
                                                  preferred_element_type=F32)
        m_ref[g] = m_new

    n_ahead = st_ref.shape[0]

    def key_block(j, diagonal):
        pending = [st_ref[a] for a in range(n_ahead)]
        for g in range(n_heads):
            ahead = g + n_ahead
            new = None
            if ahead < n_heads:
                new = scores(j, ahead, diagonal)
            elif not diagonal:
                st_ref[ahead - n_heads] = scores(j + 1, ahead - n_heads, None)
            softmax_pv(j, g, pending[0])
            pending = pending[1:] + [new]

    def body(j, carry):
        key_block(j, diagonal=False)
        return carry

    for a in range(n_ahead):
        st_ref[a] = scores(0, a, None)
    lax.fori_loop(0, qi, body, 0)
    key_block(qi, diagonal=True)

    lp = lamp_ref[...]
    s1 = jnp.sum(lp[0:1] * lp[1:2], axis=-1, keepdims=True)
    s2 = jnp.sum(lp[2:3] * lp[3:4], axis=-1, keepdims=True)
    lam = jnp.exp(s1) - jnp.exp(s2) + lambda_init
    for g in range(n_heads):
        o = acc_ref[g, 0:HEAD_W, :] / acc_ref[g, HEAD_W:HEAD_W + 1, :]
        o = o[:, 0:tq] - lam * o[:, tq:]
        ms = jnp.mean(o * o, axis=0, keepdims=True)
        y = (o * lax.rsqrt(ms + NORM_EPS)).T
        o_ref[:, head_lanes(g)] = (y * sg_ref[...] * (1.0 - lambda_init)).astype(BF16)


def _attention(qkv, lam_params, subln_g, bsz, seq, lambda_init, tq=256, heads_per_step=8,
               heads_ahead=2):
    t = qkv.shape[0]
    n_q_tiles = seq // tq
    gw = heads_per_step * HEAD_W
    n_groups = ATT_HEADS // heads_per_step
    vt_rows = HEAD_W + BF16_SUBLANES
    assert t == bsz * seq and seq % tq == 0 and tq % CHUNK == 0 and ATT_HEADS % heads_per_step == 0
    return pl.pallas_call(
        functools.partial(_attn_kernel, lambda_init=lambda_init),
        grid=(bsz, n_groups, n_q_tiles),
        in_specs=[
            pl.BlockSpec((4, ATT_QK_DIM), lambda b, hg, qi: (0, 0)),
            pl.BlockSpec((tq, gw), lambda b, hg, qi: (b * n_q_tiles + qi, hg)),
            pl.BlockSpec((seq, gw), lambda b, hg, qi: (b, n_groups + hg)),
            pl.BlockSpec((seq, gw), lambda b, hg, qi: (b, 2 * n_groups + hg)),
            pl.BlockSpec((1, HEAD_W), lambda b, hg, qi: (0, 0)),
        ],
        out_specs=pl.BlockSpec((tq, gw), lambda b, hg, qi: (b * n_q_tiles + qi, hg)),
        out_shape=jax.ShapeDtypeStruct((t, ATT_HEADS * HEAD_W), BF16),
        scratch_shapes=[
            pltpu.VMEM((heads_per_step, 2 * tq, HEAD_W), BF16),
            pltpu.VMEM((heads_per_step, seq // tq, vt_rows, tq), BF16),
            pltpu.VMEM((heads_ahead, tq, 2 * tq), F32),
            pltpu.VMEM((heads_per_step, 1, 2 * tq), F32),
            pltpu.VMEM((heads_per_step, vt_rows, 2 * tq), F32),
        ],
        compiler_params=_params("parallel", "parallel", "arbitrary"),
        name="diffattn",
    )(lam_params, qkv, qkv, qkv, subln_g.reshape(1, HEAD_W))


def _outproj_kernel(c_ref, a_ref, x_ref, wc_ref, wa_ref, g_ref, x1_ref, h_ref):
    mix = jnp.dot(c_ref[...], wc_ref[...], preferred_element_type=F32)
    mix = mix + jnp.dot(a_ref[...], wa_ref[...], preferred_element_type=F32)
    x1 = x_ref[...] + mix
    x1_ref[...] = x1
    h_ref[...] = _rmsnorm(x1, g_ref[...]).astype(BF16)


def _outproj(conv_out, attn_out, x2d, w_out, g, tm=256):
    t, d = x2d.shape
    c = conv_out.shape[1]
    a = attn_out.shape[1]
    assert w_out.shape == (c + a, d) and c == a and t % tm == 0
    return pl.pallas_call(
        _outproj_kernel,
        grid=(t // tm,),
        in_specs=[
            pl.BlockSpec((tm, c), lambda i: (i, 0)),
            pl.BlockSpec((tm, a), lambda i: (i, 0)),
            pl.BlockSpec((tm, d), lambda i: (i, 0)),
            _resident((c, d), lambda i: (0, 0)),
            _resident((a, d), lambda i: (1, 0)),
            pl.BlockSpec((1, d), lambda i: (0, 0)),
        ],
        out_specs=[pl.BlockSpec((tm, d), lambda i: (i, 0)),
                   pl.BlockSpec((tm, d), lambda i: (i, 0))],
        out_shape=[jax.ShapeDtypeStruct((t, d), F32), jax.ShapeDtypeStruct((t, d), BF16)],
        compiler_params=_params("parallel"),
        name="outproj",
    )(conv_out, attn_out, x2d, w_out, w_out, g.reshape(1, d))


def _ffnup_kernel(h_ref, wg_ref, wu_ref, cw_ref, cb_ref, o_ref, gbuf_ref, *, tiles_per_seq,
                  row_chunk):
    i = pl.program_id(1)
    tm, tn = o_ref.shape

    @pl.when(i % tiles_per_seq == 0)
    def _():
        gbuf_ref[:, 0:SUBLANES, :] = jnp.zeros((tn // LANES, SUBLANES, LANES), F32)

    for r0 in range(0, tm, row_chunk):
        rows = slice(r0, r0 + row_chunk)
        h = h_ref[rows, :]
        gate = jnp.dot(h, wg_ref[...], preferred_element_type=F32)
        up = jnp.dot(h, wu_ref[...], preferred_element_type=F32)
        for lc in range(tn // LANES):
            lanes = slice(lc * LANES, (lc + 1) * LANES)
            g0 = gate[:, lanes]
            gbuf_ref[lc, SUBLANES + r0:SUBLANES + r0 + row_chunk, :] = g0
            g1 = gbuf_ref[lc, pl.ds(SUBLANES + r0 - 1, row_chunk), :]
            g2 = gbuf_ref[lc, pl.ds(SUBLANES + r0 - 2, row_chunk), :]
            conv = (cb_ref[:, lanes] + cw_ref[2:3, lanes] * g0 + cw_ref[1:2, lanes] * g1
                    + cw_ref[0:1, lanes] * g2)
            gelu = 0.5 * conv * (1.0 + lax.erf(conv * math.sqrt(0.5)))
            o_ref[rows, lanes] = (gelu * up[:, lanes]).astype(BF16)
    gbuf_ref[:, 0:SUBLANES, :] = gbuf_ref[:, tm:tm + SUBLANES, :]


def _ffnup(h2, w_up, conv_w, conv_b, seq, tm=1024, tn=512, row_chunk=256):
    t, d = h2.shape
    d_ff = w_up.shape[1] // 2
    assert t % tm == 0 and seq % tm == 0 and d_ff % tn == 0 and conv_w.shape[0] == FFN_CONV_K
    assert tm % row_chunk == 0 and tn % LANES == 0
    n_col = d_ff // tn
    return pl.pallas_call(
        functools.partial(_ffnup_kernel, tiles_per_seq=seq // tm, row_chunk=row_chunk),
        grid=(n_col, t // tm),
        in_specs=[
            pl.BlockSpec((tm, d), lambda j, i: (i, 0)),
            pl.BlockSpec((d, tn), lambda j, i: (0, j)),
            pl.BlockSpec((d, tn), lambda j, i: (0, n_col + j)),
            pl.BlockSpec((FFN_CONV_K, tn), lambda j, i: (0, j)),
            pl.BlockSpec((1, tn), lambda j, i: (0, j)),
        ],
        out_specs=pl.BlockSpec((tm, tn), lambda j, i: (i, j)),
        out_shape=jax.ShapeDtypeStruct((t, d_ff), BF16),
        scratch_shapes=[pltpu.VMEM((tn // LANES, tm + SUBLANES, LANES), F32)],
        compiler_params=_params("parallel", "arbitrary"),
        name="ffnup",
    )(h2, w_up, w_up, conv_w, conv_b.reshape(1, d_ff))


def _ffndown_kernel(a_ref, w_ref, x1_ref, g_ref, o_ref, *, final_norm):
    x2 = x1_ref[...] + jnp.dot(a_ref[...], w_ref[...], preferred_element_type=F32)
    o_ref[...] = _rmsnorm(x2, g_ref[...]) if final_norm else x2


def _ffndown(act, w_down, x1, g, final_norm, tm=256):
    t, d_ff = act.shape
    d = w_down.shape[1]
    assert t % tm == 0
    return pl.pallas_call(
        functools.partial(_ffndown_kernel, final_norm=final_norm),
        grid=(t // tm,),
        in_specs=[
            pl.BlockSpec((tm, d_ff), lambda i: (i, 0)),
            _resident((d_ff, d), lambda i: (0, 0)),
            pl.BlockSpec((tm, d), lambda i: (i, 0)),
            pl.BlockSpec((1, d), lambda i: (0, 0)),
        ],
        out_specs=pl.BlockSpec((tm, d), lambda i: (i, 0)),
        out_shape=jax.ShapeDtypeStruct((t, d), F32),
        compiler_params=_params("parallel"),
        name="ffndown",
    )(act, w_down, x1, g.reshape(1, d))


def _layer(x2d, bsz, seq, layer_idx, is_last, norm_final_g, norm_mix_g, w_in, conv_dw_w,
           conv_dw_b, conv_ln_g, conv_ln_b, lambda_q1, lambda_k1, lambda_q2, lambda_k2, subln_g,
           w_out, norm_ffn_g, w_up, ffn_dw_w, ffn_dw_b, w_down):
    conv_ch = conv_dw_w.shape[1]
    lambda_init = 0.8 - 0.6 * math.exp(-0.3 * layer_idx)
    u, qkv = _inproj(x2d, norm_mix_g, w_in.astype(BF16), conv_ch)
    conv_out = _convbranch(u.reshape(bsz, seq, conv_ch), conv_dw_w, conv_dw_b, conv_ln_g,
                           conv_ln_b).reshape(bsz * seq, conv_ch)
    lam_params = jnp.stack([lambda_q1, lambda_k1, lambda_q2, lambda_k2]).astype(F32)
    attn_out = _attention(qkv, lam_params, subln_g, bsz, seq, lambda_init)
    x1, h2 = _outproj(conv_out, attn_out, x2d, w_out.astype(BF16), norm_ffn_g)
    act = _ffnup(h2, w_up.astype(BF16), ffn_dw_w, ffn_dw_b, seq)
    final_g = norm_final_g if is_last else jnp.ones_like(norm_final_g)
    return _ffndown(act, w_down.astype(BF16), x1, final_g, final_norm=is_last)


def kernel(x, norm_mix_g, w_in, conv_dw_w, conv_dw_b, conv_ln_g, conv_ln_b, lambda_q1, lambda_k1,
           lambda_q2, lambda_k2, subln_g, w_out, norm_ffn_g, w_up, ffn_dw_w, ffn_dw_b, w_down,
           norm_final_g):
    bsz, seq, d = x.shape
    depth = w_in.shape[0]
    x2d = x.reshape(bsz * seq, d)
    for l in range(depth):
        x2d = _layer(x2d, bsz, seq, l, l == depth - 1, norm_final_g, norm_mix_g[l], w_in[l],
                     conv_dw_w[l], conv_dw_b[l], conv_ln_g[l], conv_ln_b[l], lambda_q1[l],
                     lambda_k1[l], lambda_q2[l], lambda_k2[l], subln_g[l], w_out[l],
                     norm_ffn_g[l], w_up[l], ffn_dw_w[l], ffn_dw_b[l], w_down[l])
    return x2d.reshape(bsz, seq, d)
```

```python
import functools
import math

import jax
import jax.numpy as jnp
from jax import lax
from jax.experimental import pallas as pl
from jax.experimental.pallas import tpu as pltpu

F32 = jnp.float32
BF16 = jnp.bfloat16

CHUNK = 64
CONV_K = 31
FFN_CONV_K = 3
ATT_HEADS = 8
ATT_QK_DIM = 64
HEAD_W = 2 * ATT_QK_DIM
NORM_EPS = 1e-6
LN_EPS = 1e-5
MASK_VALUE = -1e30
Q_PRESCALE = math.log2(math.e) / math.sqrt(ATT_QK_DIM)

VMEM_LIMIT_BYTES = 56 * 1024 * 1024
SUBLANES = 8
LANES = 128
BF16_SUBLANES = 16
CONV_HALO = 32


def _params(*semantics):
    return pltpu.CompilerParams(dimension_semantics=semantics,
                                vmem_limit_bytes=VMEM_LIMIT_BYTES)


def _resident(block_shape, index_map):
    return pl.BlockSpec(block_shape, index_map, pipeline_mode=pl.Buffered(1))


def _rmsnorm(x, g):
    ms = jnp.mean(x * x, axis=-1, keepdims=True)
    return x * lax.rsqrt(ms + NORM_EPS) * g


def _sigmoid(x):
    return 1.0 / (1.0 + jnp.exp(-x))


def _inproj_kernel(x_ref, g_ref, w_ref, u_ref, qkv_ref, *, row_chunk):
    tm, n = u_ref.shape
    n_qkv = qkv_ref.shape[1] // n
    for r0 in range(0, tm, row_chunk):
        rows = slice(r0, r0 + row_chunk)
        h = _rmsnorm(x_ref[rows, :], g_ref[...]).astype(BF16)
        a = jnp.dot(h, w_ref[:, 0:n], preferred_element_type=F32)
        gate = jnp.dot(h, w_ref[:, n:2 * n], preferred_element_type=F32)
        u_ref[rows, :] = a * _sigmoid(gate)
        for c in range(n_qkv):
            z = jnp.dot(h, w_ref[:, (2 + c) * n:(3 + c) * n], preferred_element_type=F32)
            if c == 0:
                z = z * Q_PRESCALE
            qkv_ref[rows, c * n:(c + 1) * n] = z.astype(BF16)


def _inproj(x2d, g, w_in, conv_ch, tm=512, row_chunk=256):
    t, d = x2d.shape
    n = conv_ch
    n_groups = w_in.shape[1] // n
    assert w_in.shape[1] == n_groups * n and t % tm == 0 and tm % row_chunk == 0
    return pl.pallas_call(
        functools.partial(_inproj_kernel, row_chunk=row_chunk),
        grid=(t // tm,),
        in_specs=[
            pl.BlockSpec((tm, d), lambda i: (i, 0)),
            pl.BlockSpec((1, d), lambda i: (0, 0)),
            _resident((d, n_groups * n), lambda i: (0, 0)),
        ],
        out_specs=[
            pl.BlockSpec((tm, n), lambda i: (i, 0)),
            pl.BlockSpec((tm, (n_groups - 2) * n), lambda i: (i, 0)),
        ],
        out_shape=[
            jax.ShapeDtypeStruct((t, n), F32),
            jax.ShapeDtypeStruct((t, (n_groups - 2) * n), BF16),
        ],
        compiler_params=_params("parallel"),
        name="inproj",
    )(x2d, g.reshape(1, d), w_in)


def _convbranch_kernel(u_ref, halo_ref, w_ref, b_ref, lg_ref, lb_ref, o_ref, win_ref, y_ref):
    t = pl.program_id(1)
    ts, c = u_ref.shape[1], u_ref.shape[2]
    n_chunks = c // LANES
    n_groups = ts // SUBLANES
    first_tap_row = CONV_HALO - (CONV_K - 1)

    for lc in range(n_chunks):
        lanes = slice(lc * LANES, (lc + 1) * LANES)
        halo = halo_ref[0, :, lanes]
        win_ref[lc, 0:CONV_HALO, :] = jnp.where(t > 0, halo, jnp.zeros_like(halo))
        win_ref[lc, CONV_HALO:, :] = u_ref[0, :, lanes]

    def conv_chunk(lc, carry):
        bias = jnp.broadcast_to(b_ref[lc], (SUBLANES, LANES))
        accs = [bias for _ in range(n_groups)]
        for b in range(SUBLANES):
            taps = list(range(b, CONV_K, SUBLANES))
            wks = [jnp.broadcast_to(w_ref[lc, k:k + 1, :], (SUBLANES, LANES)) for k in taps]
            for j in range(n_groups + len(taps) - 1):
                xw = win_ref[lc, pl.ds(first_tap_row + b + SUBLANES * j, SUBLANES), :]
                for a, wk in enumerate(wks):
                    g = j - a
                    if 0 <= g < n_groups:
                        accs[g] = accs[g] + xw * wk
        for g in range(n_groups):
            y_ref[lc, pl.ds(SUBLANES * g, SUBLANES), :] = accs[g]
        return carry

    lax.fori_loop(0, n_chunks, conv_chunk, 0)

    y = y_ref[...]
    mu = jnp.sum(jnp.sum(y, axis=0), axis=-1, keepdims=True) * (1.0 / c)
    yc = y - mu
    var = jnp.sum(jnp.sum(yc * yc, axis=0), axis=-1, keepdims=True) * (1.0 / c)
    z = yc * lax.rsqrt(var + LN_EPS) * lg_ref[...] + lb_ref[...]
    act = (z * _sigmoid(z)).astype(BF16)
    for lc in range(n_chunks):
        o_ref[0, :, lc * LANES:(lc + 1) * LANES] = act[lc]


def _convbranch(u, w, b, ln_g, ln_b, ts=256):
    bsz, seq, c = u.shape
    assert seq % ts == 0 and ts % CONV_HALO == 0 and c % LANES == 0 and w.shape[0] == CONV_K
    halo_blocks = ts // CONV_HALO
    n_chunks = c // LANES

    def per_chunk(p):
        return p.reshape(-1, n_chunks, LANES).transpose(1, 0, 2)

    def whole(rows):
        return pl.BlockSpec((n_chunks, rows, LANES), lambda bi, ti: (0, 0, 0))

    return pl.pallas_call(
        _convbranch_kernel,
        grid=(bsz, seq // ts),
        in_specs=[
            pl.BlockSpec((1, ts, c), lambda bi, ti: (bi, ti, 0)),
            pl.BlockSpec((1, CONV_HALO, c),
                         lambda bi, ti: (bi, jnp.maximum(ti * halo_blocks - 1, 0), 0)),
            whole(CONV_K), whole(1), whole(1), whole(1),
        ],
        out_specs=pl.BlockSpec((1, ts, c), lambda bi, ti: (bi, ti, 0)),
        out_shape=jax.ShapeDtypeStruct((bsz, seq, c), BF16),
        scratch_shapes=[pltpu.VMEM((n_chunks, ts + CONV_HALO, LANES), F32),
                        pltpu.VMEM((n_chunks, ts, LANES), F32)],
        compiler_params=_params("parallel", "parallel"),
        name="convbranch",
    )(u, u, per_chunk(w), per_chunk(b), per_chunk(ln_g), per_chunk(ln_b))


def _attn_kernel(lamp_ref, q_ref, k_ref, v_ref, sg_ref, o_ref, qz_ref, vt_ref, st_ref, m_ref,
                 acc_ref, *, lambda_init):
    qi = pl.program_id(2)
    tq = q_ref.shape[0]
    n_heads, n_blocks, vt_rows, tk = vt_ref.shape
    assert tk == tq

    def head_lanes(g):
        return slice(g * HEAD_W, (g + 1) * HEAD_W)

    @pl.when(qi == 0)
    def _():
        pad_row = lax.broadcasted_iota(jnp.int32, (vt_rows - HEAD_W, tk), 0)
        ones_then_zeros = jnp.where(pad_row == 0, 1.0, 0.0).astype(BF16)
        for g in range(n_heads):
            for j in range(n_blocks):
                vt_ref[g, j, 0:HEAD_W, :] = (
                    v_ref[j * tk:(j + 1) * tk, head_lanes(g)].astype(F32).T.astype(BF16))
                vt_ref[g, j, HEAD_W:, :] = ones_then_zeros

    for g in range(n_heads):
        q = q_ref[:, head_lanes(g)]
        lane = lax.broadcasted_iota(jnp.int32, q.shape, 1)
        zero = jnp.zeros_like(q)
        qz_ref[g, 0:tq, :] = jnp.where(lane < ATT_QK_DIM, q, zero)
        qz_ref[g, tq:, :] = jnp.where(lane >= ATT_QK_DIM, q, zero)
    m_ref[...] = jnp.full(m_ref.shape, -jnp.inf, F32)
    acc_ref[...] = jnp.zeros(acc_ref.shape, F32)

    def scores(j, g, diagonal):
        k0 = pl.multiple_of(j * tk, tk)
        kb = k_ref[pl.ds(k0, tk), head_lanes(g)]
        st = lax.dot_general(kb, qz_ref[g], (((1,), (1,)), ((), ())),
                             preferred_element_type=F32)
        if diagonal is not False:
            key = lax.broadcasted_iota(jnp.int32, st.shape, 0)
            col = lax.broadcasted_iota(jnp.int32, st.shape, 1)
            q_chunk = jnp.where(col >= tq, col - tq, col) // CHUNK
            allowed = key // CHUNK <= q_chunk
            if diagonal is None:
                allowed = jnp.logical_or(allowed, j != qi)
            st = jnp.where(allowed, st, MASK_VALUE)
        return st

    def softmax_pv(j, g, st):
        m_prev = m_ref[g]
        m_new = jnp.maximum(m_prev, jnp.max(st, axis=0, keepdims=True))
        alpha = jnp.exp2(m_prev - m_new)
        p = jnp.exp2(st - m_new)
        acc_ref[g] = alpha * acc_ref[g] + jnp.dot(vt_ref[g, j], p.astype(BF16),
                                                  preferred_element_type=F32)
        m_ref[g] = m_new

    n_ahead = st_ref.shape[0]

    def key_block(j, diagonal):
        pending = [st_ref[a] for a in range(n_ahead)]
        for g in range(n_heads):
            ahead = g + n_ahead
            new = None
            if ahead < n_heads:
                new = scores(j, ahead, diagonal)
            elif not diagonal:
                st_ref[ahead - n_heads] = scores(j + 1, ahead - n_heads, None)
            softmax_pv(j, g, pending[0])
            pending = pending[1:] + [new]

    def body(j, carry):
        key_block(j, diagonal=False)
        return carry

    for a in range(n_ahead):
        st_ref[a] = scores(0, a, None)
    lax.fori_loop(0, qi, body, 0)
    key_block(qi, diagonal=True)

    lp = lamp_ref[...]
    s1 = jnp.sum(lp[0:1] * lp[1:2], axis=-1, keepdims=True)
    s2 = jnp.sum(lp[2:3] * lp[3:4], axis=-1, keepdims=True)
    lam = jnp.exp(s1) - jnp.exp(s2) + lambda_init
    for g in range(n_heads):
        inv_l = 1.0 / acc_ref[g, HEAD_W:HEAD_W + 1, :]
        o = acc_ref[g, 0:HEAD_W, :] * inv_l
        o = o[:, 0:tq] - lam * o[:, tq:]
        ms = jnp.mean(o * o, axis=0, keepdims=True)
        y = (o * lax.rsqrt(ms + NORM_EPS)).T
        o_ref[:, head_lanes(g)] = (y * sg_ref[...] * (1.0 - lambda_init)).astype(BF16)


def _attention(qkv, lam_params, subln_g, bsz, seq, lambda_init, tq=256, heads_per_step=8,
               heads_ahead=2):
    t = qkv.shape[0]
    n_q_tiles = seq // tq
    gw = heads_per_step * HEAD_W
    n_groups = ATT_HEADS // heads_per_step
    vt_rows = HEAD_W + BF16_SUBLANES
    assert t == bsz * seq and seq % tq == 0 and tq % CHUNK == 0 and ATT_HEADS % heads_per_step == 0
    return pl.pallas_call(
        functools.partial(_attn_kernel, lambda_init=lambda_init),
        grid=(bsz, n_groups, n_q_tiles),
        in_specs=[
            pl.BlockSpec((4, ATT_QK_DIM), lambda b, hg, qi: (0, 0)),
            pl.BlockSpec((tq, gw), lambda b, hg, qi: (b * n_q_tiles + qi, hg)),
            pl.BlockSpec((seq, gw), lambda b, hg, qi: (b, n_groups + hg)),
            pl.BlockSpec((seq, gw), lambda b, hg, qi: (b, 2 * n_groups + hg)),
            pl.BlockSpec((1, HEAD_W), lambda b, hg, qi: (0, 0)),
        ],
        out_specs=pl.BlockSpec((tq, gw), lambda b, hg, qi: (b * n_q_tiles + qi, hg)),
        out_shape=jax.ShapeDtypeStruct((t, ATT_HEADS * HEAD_W), BF16),
        scratch_shapes=[
            pltpu.VMEM((heads_per_step, 2 * tq, HEAD_W), BF16),
            pltpu.VMEM((heads_per_step, seq // tq, vt_rows, tq), BF16),
            pltpu.VMEM((heads_ahead, tq, 2 * tq), F32),
            pltpu.VMEM((heads_per_step, 1, 2 * tq), F32),
            pltpu.VMEM((heads_per_step, vt_rows, 2 * tq), F32),
        ],
        compiler_params=_params("parallel", "parallel", "arbitrary"),
        name="diffattn",
    )(lam_params, qkv, qkv, qkv, subln_g.reshape(1, HEAD_W))


def _outproj_kernel(c_ref, a_ref, x_ref, wc_ref, wa_ref, g_ref, x1_ref, h_ref, *, row_chunk):
    for r0 in range(0, x_ref.shape[0], row_chunk):
        rows = slice(r0, r0 + row_chunk)
        mix = jnp.dot(c_ref[rows, :], wc_ref[...], preferred_element_type=F32)
        mix = mix + jnp.dot(a_ref[rows, :], wa_ref[...], preferred_element_type=F32)
        x1 = x_ref[rows, :] + mix
        x1_ref[rows, :] = x1
        h_ref[rows, :] = _rmsnorm(x1, g_ref[...]).astype(BF16)


def _outproj(conv_out, attn_out, x2d, w_out, g, tm=512, row_chunk=256):
    t, d = x2d.shape
    c = conv_out.shape[1]
    a = attn_out.shape[1]
    assert w_out.shape == (c + a, d) and c == a and t % tm == 0 and tm % row_chunk == 0
    return pl.pallas_call(
        functools.partial(_outproj_kernel, row_chunk=row_chunk),
        grid=(t // tm,),
        in_specs=[
            pl.BlockSpec((tm, c), lambda i: (i, 0)),
            pl.BlockSpec((tm, a), lambda i: (i, 0)),
            pl.BlockSpec((tm, d), lambda i: (i, 0)),
            _resident((c, d), lambda i: (0, 0)),
            _resident((a, d), lambda i: (1, 0)),
            pl.BlockSpec((1, d), lambda i: (0, 0)),
        ],
        out_specs=[pl.BlockSpec((tm, d), lambda i: (i, 0)),
                   pl.BlockSpec((tm, d), lambda i: (i, 0))],
        out_shape=[jax.ShapeDtypeStruct((t, d), F32), jax.ShapeDtypeStruct((t, d), BF16)],
        compiler_params=_params("parallel"),
        name="outproj",
    )(conv_out, attn_out, x2d, w_out, w_out, g.reshape(1, d))


def _ffnup_kernel(h_ref, wg_ref, wu_ref, cw_ref, cb_ref, o_ref, wgb_ref, wub_ref, gbuf_ref, *,
                  tiles_per_seq, row_chunk):
    i = pl.program_id(1)
    tm, tn = o_ref.shape

    @pl.when(i == 0)
    def _():
        wgb_ref[...] = wg_ref[...].astype(BF16)
        wub_ref[...] = wu_ref[...].astype(BF16)

    @pl.when(i % tiles_per_seq == 0)
    def _():
        gbuf_ref[:, 0:SUBLANES, :] = jnp.zeros((tn // LANES, SUBLANES, LANES), F32)

    for r0 in range(0, tm, row_chunk):
        rows = slice(r0, r0 + row_chunk)
        h = h_ref[rows, :]
        gate = jnp.dot(h, wgb_ref[...], preferred_element_type=F32)
        up = jnp.dot(h, wub_ref[...], preferred_element_type=F32)
        for lc in range(tn // LANES):
            lanes = slice(lc * LANES, (lc + 1) * LANES)
            g0 = gate[:, lanes]
            gbuf_ref[lc, SUBLANES + r0:SUBLANES + r0 + row_chunk, :] = g0
            g1 = gbuf_ref[lc, pl.ds(SUBLANES + r0 - 1, row_chunk), :]
            g2 = gbuf_ref[lc, pl.ds(SUBLANES + r0 - 2, row_chunk), :]
            conv = (cb_ref[:, lanes] + cw_ref[2:3, lanes] * g0 + cw_ref[1:2, lanes] * g1
                    + cw_ref[0:1, lanes] * g2)
            gelu = 0.5 * conv * (1.0 + lax.erf(conv * math.sqrt(0.5)))
            o_ref[rows, lanes] = (gelu * up[:, lanes]).astype(BF16)
    gbuf_ref[:, 0:SUBLANES, :] = gbuf_ref[:, tm:tm + SUBLANES, :]


def _ffnup(h2, w_up, conv_w, conv_b, seq, tm=2048, tn=512, row_chunk=256):
    t, d = h2.shape
    d_ff = w_up.shape[1] // 2
    assert t % tm == 0 and seq % tm == 0 and d_ff % tn == 0 and conv_w.shape[0] == FFN_CONV_K
    assert tm % row_chunk == 0 and tn % LANES == 0
    n_col = d_ff // tn
    return pl.pallas_call(
        functools.partial(_ffnup_kernel, tiles_per_seq=seq // tm, row_chunk=row_chunk),
        grid=(n_col, t // tm),
        in_specs=[
            pl.BlockSpec((tm, d), lambda j, i: (i, 0)),
            pl.BlockSpec((d, tn), lambda j, i: (0, j)),
            pl.BlockSpec((d, tn), lambda j, i: (0, n_col + j)),
            pl.BlockSpec((FFN_CONV_K, tn), lambda j, i: (0, j)),
            pl.BlockSpec((1, tn), lambda j, i: (0, j)),
        ],
        out_specs=pl.BlockSpec((tm, tn), lambda j, i: (i, j)),
        out_shape=jax.ShapeDtypeStruct((t, d_ff), BF16),
        scratch_shapes=[pltpu.VMEM((d, tn), BF16), pltpu.VMEM((d, tn), BF16),
                        pltpu.VMEM((tn // LANES, tm + SUBLANES, LANES), F32)],
        compiler_params=_params("parallel", "arbitrary"),
        name="ffnup",
    )(h2, w_up, w_up, conv_w, conv_b.reshape(1, d_ff))


def _ffndown_kernel(a_ref, w_ref, x1_ref, g_ref, o_ref, *, final_norm, row_chunk):
    for r0 in range(0, a_ref.shape[0], row_chunk):
        rows = slice(r0, r0 + row_chunk)
        x2 = x1_ref[rows, :] + jnp.dot(a_ref[rows, :], w_ref[...], preferred_element_type=F32)
        o_ref[rows, :] = _rmsnorm(x2, g_ref[...]) if final_norm else x2


def _ffndown(act, w_down, x1, g, final_norm, tm=512, row_chunk=256):
    t, d_ff = act.shape
    d = w_down.shape[1]
    assert t % tm == 0 and tm % row_chunk == 0
    return pl.pallas_call(
        functools.partial(_ffndown_kernel, final_norm=final_norm, row_chunk=row_chunk),
        grid=(t // tm,),
        in_specs=[
            pl.BlockSpec((tm, d_ff), lambda i: (i, 0)),
            _resident((d_ff, d), lambda i: (0, 0)),
            pl.BlockSpec((tm, d), lambda i: (i, 0)),
            pl.BlockSpec((1, d), lambda i: (0, 0)),
        ],
        out_specs=pl.BlockSpec((tm, d), lambda i: (i, 0)),
        out_shape=jax.ShapeDtypeStruct((t, d), F32),
        compiler_params=_params("parallel"),
        name="ffndown",
    )(act, w_down, x1, g.reshape(1, d))


def _layer(x2d, bsz, seq, layer_idx, is_last, norm_final_g, norm_mix_g, w_in, conv_dw_w,
           conv_dw_b, conv_ln_g, conv_ln_b, lambda_q1, lambda_k1, lambda_q2, lambda_k2, subln_g,
           w_out, norm_ffn_g, w_up, ffn_dw_w, ffn_dw_b, w_down):
    conv_ch = conv_dw_w.shape[1]
    lambda_init = 0.8 - 0.6 * math.exp(-0.3 * layer_idx)
    u, qkv = _inproj(x2d, norm_mix_g, w_in.astype(BF16), conv_ch)
    conv_out = _convbranch(u.reshape(bsz, seq, conv_ch), conv_dw_w, conv_dw_b, conv_ln_g,
                           conv_ln_b).reshape(bsz * seq, conv_ch)
    lam_params = jnp.stack([lambda_q1, lambda_k1, lambda_q2, lambda_k2]).astype(F32)
    attn_out = _attention(qkv, lam_params, subln_g, bsz, seq, lambda_init)
    x1, h2 = _outproj(conv_out, attn_out, x2d, w_out.astype(BF16), norm_ffn_g)
    act = _ffnup(h2, w_up, ffn_dw_w, ffn_dw_b, seq)
    final_g = norm_final_g if is_last else jnp.ones_like(norm_final_g)
    return _ffndown(act, w_down.astype(BF16), x1, final_g, final_norm=is_last)


def kernel(x, norm_mix_g, w_in, conv_dw_w, conv_dw_b, conv_ln_g, conv_ln_b, lambda_q1, lambda_k1,
           lambda_q2, lambda_k2, subln_g, w_out, norm_ffn_g, w_up, ffn_dw_w, ffn_dw_b, w_down,
           norm_final_g):
    bsz, seq, d = x.shape
    depth = w_in.shape[0]
    x2d = x.reshape(bsz * seq, d)
    for l in range(depth):
        x2d = _layer(x2d, bsz, seq, l, l == depth - 1, norm_final_g, norm_mix_g[l], w_in[l],
                     conv_dw_w[l], conv_dw_b[l], conv_ln_g[l], conv_ln_b[l], lambda_q1[l],
                     lambda_k1[l], lambda_q2[l], lambda_k2[l], subln_g[l], w_out[l],
                     norm_ffn_g[l], w_up[l], ffn_dw_w[l], ffn_dw_b[l], w_down[l])
    return x2d.reshape(bsz, seq, d)
```

```python
import functools
import math

import jax
import jax.numpy as jnp
from jax import lax
from jax.experimental import pallas as pl
from jax.experimental.pallas import tpu as pltpu

F32 = jnp.float32
BF16 = jnp.bfloat16

CHUNK = 64
CONV_K = 31
FFN_CONV_K = 3
ATT_HEADS = 8
ATT_QK_DIM = 64
HEAD_W = 2 * ATT_QK_DIM
NORM_EPS = 1e-6
LN_EPS = 1e-5
MASK_VALUE = -1e30
Q_PRESCALE = math.log2(math.e) / math.sqrt(ATT_QK_DIM)

VMEM_LIMIT_BYTES = 56 * 1024 * 1024
SUBLANES = 8
LANES = 128
BF16_SUBLANES = 16
CONV_HALO = 32


def _params(*semantics):
    return pltpu.CompilerParams(dimension_semantics=semantics,
                                vmem_limit_bytes=VMEM_LIMIT_BYTES)


def _resident(block_shape, index_map):
    return pl.BlockSpec(block_shape, index_map, pipeline_mode=pl.Buffered(1))


def _rmsnorm(x, g):
    ms = jnp.mean(x * x, axis=-1, keepdims=True)
    return x * lax.rsqrt(ms + NORM_EPS) * g


def _sigmoid(x):
    return 1.0 / (1.0 + jnp.exp(-x))


def _inproj_kernel(x_ref, g_ref, w_ref, u_ref, qkv_ref, *, row_chunk):
    tm, n = u_ref.shape
    n_qkv = qkv_ref.shape[1] // n
    for r0 in range(0, tm, row_chunk):
        rows = slice(r0, r0 + row_chunk)
        h = _rmsnorm(x_ref[rows, :], g_ref[...]).astype(BF16)
        a = jnp.dot(h, w_ref[:, 0:n], preferred_element_type=F32)
        gate = jnp.dot(h, w_ref[:, n:2 * n], preferred_element_type=F32)
        u_ref[rows, :] = a * _sigmoid(gate)
        for c in range(n_qkv):
            z = jnp.dot(h, w_ref[:, (2 + c) * n:(3 + c) * n], preferred_element_type=F32)
            if c == 0:
                z = z * Q_PRESCALE
            qkv_ref[rows, c * n:(c + 1) * n] = z.astype(BF16)


def _inproj(x2d, g, w_in, conv_ch, tm=512, row_chunk=256):
    t, d = x2d.shape
    n = conv_ch
    n_groups = w_in.shape[1] // n
    assert w_in.shape[1] == n_groups * n and t % tm == 0 and tm % row_chunk == 0
    return pl.pallas_call(
        functools.partial(_inproj_kernel, row_chunk=row_chunk),
        grid=(t // tm,),
        in_specs=[
            pl.BlockSpec((tm, d), lambda i: (i, 0)),
            pl.BlockSpec((1, d), lambda i: (0, 0)),
            _resident((d, n_groups * n), lambda i: (0, 0)),
        ],
        out_specs=[
            pl.BlockSpec((tm, n), lambda i: (i, 0)),
            pl.BlockSpec((tm, (n_groups - 2) * n), lambda i: (i, 0)),
        ],
        out_shape=[
            jax.ShapeDtypeStruct((t, n), F32),
            jax.ShapeDtypeStruct((t, (n_groups - 2) * n), BF16),
        ],
        compiler_params=_params("parallel"),
        name="inproj",
    )(x2d, g.reshape(1, d), w_in)


def _convbranch_kernel(u_ref, halo_ref, w_ref, b_ref, lg_ref, lb_ref, o_ref, win_ref, y_ref):
    t = pl.program_id(1)
    ts, c = u_ref.shape[1], u_ref.shape[2]
    n_chunks = c // LANES
    n_groups = ts // SUBLANES
    first_tap_row = CONV_HALO - (CONV_K - 1)

    for lc in range(n_chunks):
        lanes = slice(lc * LANES, (lc + 1) * LANES)
        halo = halo_ref[0, :, lanes]
        win_ref[lc, 0:CONV_HALO, :] = jnp.where(t > 0, halo, jnp.zeros_like(halo))
        win_ref[lc, CONV_HALO:, :] = u_ref[0, :, lanes]

    def conv_chunk(lc, carry):
        bias = jnp.broadcast_to(b_ref[lc], (SUBLANES, LANES))
        accs = [bias for _ in range(n_groups)]
        for b in range(SUBLANES):
            taps = list(range(b, CONV_K, SUBLANES))
            wks = [jnp.broadcast_to(w_ref[lc, k:k + 1, :], (SUBLANES, LANES)) for k in taps]
            for j in range(n_groups + len(taps) - 1):
                xw = win_ref[lc, pl.ds(first_tap_row + b + SUBLANES * j, SUBLANES), :]
                for a, wk in enumerate(wks):
                    g = j - a
                    if 0 <= g < n_groups:
                        accs[g] = accs[g] + xw * wk
        for g in range(n_groups):
            y_ref[lc, pl.ds(SUBLANES * g, SUBLANES), :] = accs[g]
        return carry

    lax.fori_loop(0, n_chunks, conv_chunk, 0)

    y = y_ref[...]
    mu = jnp.sum(jnp.sum(y, axis=0), axis=-1, keepdims=True) * (1.0 / c)
    yc = y - mu
    var = jnp.sum(jnp.sum(yc * yc, axis=0), axis=-1, keepdims=True) * (1.0 / c)
    z = yc * lax.rsqrt(var + LN_EPS) * lg_ref[...] + lb_ref[...]
    act = (z * _sigmoid(z)).astype(BF16)
    for lc in range(n_chunks):
        o_ref[0, :, lc * LANES:(lc + 1) * LANES] = act[lc]


def _convbranch(u, w, b, ln_g, ln_b, ts=256):
    bsz, seq, c = u.shape
    assert seq % ts == 0 and ts % CONV_HALO == 0 and c % LANES == 0 and w.shape[0] == CONV_K
    halo_blocks = ts // CONV_HALO
    n_chunks = c // LANES

    def per_chunk(p):
        return p.reshape(-1, n_chunks, LANES).transpose(1, 0, 2)

    def whole(rows):
        return pl.BlockSpec((n_chunks, rows, LANES), lambda bi, ti: (0, 0, 0))

    return pl.pallas_call(
        _convbranch_kernel,
        grid=(bsz, seq // ts),
        in_specs=[
            pl.BlockSpec((1, ts, c), lambda bi, ti: (bi, ti, 0)),
            pl.BlockSpec((1, CONV_HALO, c),
                         lambda bi, ti: (bi, jnp.maximum(ti * halo_blocks - 1, 0), 0)),
            whole(CONV_K), whole(1), whole(1), whole(1),
        ],
        out_specs=pl.BlockSpec((1, ts, c), lambda bi, ti: (bi, ti, 0)),
        out_shape=jax.ShapeDtypeStruct((bsz, seq, c), BF16),
        scratch_shapes=[pltpu.VMEM((n_chunks, ts + CONV_HALO, LANES), F32),
                        pltpu.VMEM((n_chunks, ts, LANES), F32)],
        compiler_params=_params("parallel", "parallel"),
        name="convbranch",
    )(u, u, per_chunk(w), per_chunk(b), per_chunk(ln_g), per_chunk(ln_b))


def _attn_kernel(lamp_ref, q_ref, k_ref, v_ref, sg_ref, o_ref, qz_ref, vt_ref, st_ref, smax_ref,
                 m_ref, acc_ref, *, lambda_init):
    qi = pl.program_id(2)
    tq = q_ref.shape[0]
    n_heads, n_blocks, vt_rows, tk = vt_ref.shape
    assert tk == tq

    def head_lanes(g):
        return slice(g * HEAD_W, (g + 1) * HEAD_W)

    @pl.when(qi == 0)
    def _():
        pad_row = lax.broadcasted_iota(jnp.int32, (vt_rows - HEAD_W, tk), 0)
        ones_then_zeros = jnp.where(pad_row == 0, 1.0, 0.0).astype(BF16)
        for g in range(n_heads):
            for j in range(n_blocks):
                vt_ref[g, j, 0:HEAD_W, :] = (
                    v_ref[j * tk:(j + 1) * tk, head_lanes(g)].astype(F32).T.astype(BF16))
                vt_ref[g, j, HEAD_W:, :] = ones_then_zeros

    for g in range(n_heads):
        q = q_ref[:, head_lanes(g)]
        lane = lax.broadcasted_iota(jnp.int32, q.shape, 1)
        zero = jnp.zeros_like(q)
        qz_ref[g, 0:tq, :] = jnp.where(lane < ATT_QK_DIM, q, zero)
        qz_ref[g, tq:, :] = jnp.where(lane >= ATT_QK_DIM, q, zero)
    m_ref[...] = jnp.full(m_ref.shape, -jnp.inf, F32)
    acc_ref[...] = jnp.zeros(acc_ref.shape, F32)

    def scores(j, g, diagonal):
        k0 = pl.multiple_of(j * tk, tk)
        kb = k_ref[pl.ds(k0, tk), head_lanes(g)]
        st = lax.dot_general(kb, qz_ref[g], (((1,), (1,)), ((), ())),
                             preferred_element_type=F32)
        if diagonal is not False:
            key = lax.broadcasted_iota(jnp.int32, st.shape, 0)
            col = lax.broadcasted_iota(jnp.int32, st.shape, 1)
            q_chunk = jnp.where(col >= tq, col - tq, col) // CHUNK
            allowed = key // CHUNK <= q_chunk
            if diagonal is None:
                allowed = jnp.logical_or(allowed, j != qi)
            st = jnp.where(allowed, st, MASK_VALUE)
        return st, jnp.max(st, axis=0, keepdims=True)

    def softmax_pv(j, g, st_and_max):
        st, m_blk = st_and_max
        m_prev = m_ref[g]
        m_new = jnp.maximum(m_prev, m_blk)
        alpha = jnp.exp2(m_prev - m_new)
        p = jnp.exp2(st - m_new)
        acc_ref[g] = alpha * acc_ref[g] + jnp.dot(vt_ref[g, j], p.astype(BF16),
                                                  preferred_element_type=F32)
        m_ref[g] = m_new

    n_ahead = st_ref.shape[0]

    def key_block(j, diagonal):
        pending = [(st_ref[a], smax_ref[a]) for a in range(n_ahead)]
        for g in range(n_heads):
            ahead = g + n_ahead
            new = None
            if ahead < n_heads:
                new = scores(j, ahead, diagonal)
            elif not diagonal:
                st_ref[ahead - n_heads], smax_ref[ahead - n_heads] = scores(
                    j + 1, ahead - n_heads, None)
            softmax_pv(j, g, pending[0])
            pending = pending[1:] + [new]

    def body(j, carry):
        key_block(j, diagonal=False)
        return carry

    for a in range(n_ahead):
        st_ref[a], smax_ref[a] = scores(0, a, None)
    lax.fori_loop(0, qi, body, 0)
    key_block(qi, diagonal=True)

    lp = lamp_ref[...]
    s1 = jnp.sum(lp[0:1] * lp[1:2], axis=-1, keepdims=True)
    s2 = jnp.sum(lp[2:3] * lp[3:4], axis=-1, keepdims=True)
    lam = jnp.exp(s1) - jnp.exp(s2) + lambda_init
    for g in range(n_heads):
        inv_l = 1.0 / acc_ref[g, HEAD_W:HEAD_W + 1, :]
        o = acc_ref[g, 0:HEAD_W, :] * inv_l
        o = o[:, 0:tq] - lam * o[:, tq:]
        ms = jnp.mean(o * o, axis=0, keepdims=True)
        y = (o * lax.rsqrt(ms + NORM_EPS)).T
        o_ref[:, head_lanes(g)] = (y * sg_ref[...] * (1.0 - lambda_init)).astype(BF16)


def _attention(qkv, lam_params, subln_g, bsz, seq, lambda_init, tq=256, heads_per_step=8,
               heads_ahead=2):
    t = qkv.shape[0]
    n_q_tiles = seq // tq
    gw = heads_per_step * HEAD_W
    n_groups = ATT_HEADS // heads_per_step
    vt_rows = HEAD_W + BF16_SUBLANES
    assert t == bsz * seq and seq % tq == 0 and tq % CHUNK == 0 and ATT_HEADS % heads_per_step == 0
    return pl.pallas_call(
        functools.partial(_attn_kernel, lambda_init=lambda_init),
        grid=(bsz, n_groups, n_q_tiles),
        in_specs=[
            pl.BlockSpec((4, ATT_QK_DIM), lambda b, hg, qi: (0, 0)),
            pl.BlockSpec((tq, gw), lambda b, hg, qi: (b * n_q_tiles + qi, hg)),
            pl.BlockSpec((seq, gw), lambda b, hg, qi: (b, n_groups + hg)),
            pl.BlockSpec((seq, gw), lambda b, hg, qi: (b, 2 * n_groups + hg)),
            pl.BlockSpec((1, HEAD_W), lambda b, hg, qi: (0, 0)),
        ],
        out_specs=pl.BlockSpec((tq, gw), lambda b, hg, qi: (b * n_q_tiles + qi, hg)),
        out_shape=jax.ShapeDtypeStruct((t, ATT_HEADS * HEAD_W), BF16),
        scratch_shapes=[
            pltpu.VMEM((heads_per_step, 2 * tq, HEAD_W), BF16),
            pltpu.VMEM((heads_per_step, seq // tq, vt_rows, tq), BF16),
            pltpu.VMEM((heads_ahead, tq, 2 * tq), F32),
            pltpu.VMEM((heads_ahead, 1, 2 * tq), F32),
            pltpu.VMEM((heads_per_step, 1, 2 * tq), F32),
            pltpu.VMEM((heads_per_step, vt_rows, 2 * tq), F32),
        ],
        compiler_params=_params("parallel", "parallel", "arbitrary"),
        name="diffattn",
    )(lam_params, qkv, qkv, qkv, subln_g.reshape(1, HEAD_W))


def _outproj_kernel(c_ref, a_ref, x_ref, wc_ref, wa_ref, g_ref, x1_ref, h_ref, *, row_chunk):
    for r0 in range(0, x_ref.shape[0], row_chunk):
        rows = slice(r0, r0 + row_chunk)
        mix = jnp.dot(c_ref[rows, :], wc_ref[...], preferred_element_type=F32)
        mix = mix + jnp.dot(a_ref[rows, :], wa_ref[...], preferred_element_type=F32)
        x1 = x_ref[rows, :] + mix
        x1_ref[rows, :] = x1
        h_ref[rows, :] = _rmsnorm(x1, g_ref[...]).astype(BF16)


def _outproj(conv_out, attn_out, x2d, w_out, g, tm=512, row_chunk=256):
    t, d = x2d.shape
    c = conv_out.shape[1]
    a = attn_out.shape[1]
    assert w_out.shape == (c + a, d) and c == a and t % tm == 0 and tm % row_chunk == 0
    return pl.pallas_call(
        functools.partial(_outproj_kernel, row_chunk=row_chunk),
        grid=(t // tm,),
        in_specs=[
            pl.BlockSpec((tm, c), lambda i: (i, 0)),
            pl.BlockSpec((tm, a), lambda i: (i, 0)),
            pl.BlockSpec((tm, d), lambda i: (i, 0)),
            _resident((c, d), lambda i: (0, 0)),
            _resident((a, d), lambda i: (1, 0)),
            pl.BlockSpec((1, d), lambda i: (0, 0)),
        ],
        out_specs=[pl.BlockSpec((tm, d), lambda i: (i, 0)),
                   pl.BlockSpec((tm, d), lambda i: (i, 0))],
        out_shape=[jax.ShapeDtypeStruct((t, d), F32), jax.ShapeDtypeStruct((t, d), BF16)],
        compiler_params=_params("parallel"),
        name="outproj",
    )(conv_out, attn_out, x2d, w_out, w_out, g.reshape(1, d))


def _ffnup_kernel(h_ref, wg_ref, wu_ref, cw_ref, cb_ref, o_ref, wgb_ref, wub_ref, gbuf_ref, *,
                  tiles_per_seq, row_chunk):
    i = pl.program_id(1)
    tm, tn = o_ref.shape

    @pl.when(i == 0)
    def _():
        wgb_ref[...] = wg_ref[...].astype(BF16)
        wub_ref[...] = wu_ref[...].astype(BF16)

    @pl.when(i % tiles_per_seq == 0)
    def _():
        gbuf_ref[:, 0:SUBLANES, :] = jnp.zeros((tn // LANES, SUBLANES, LANES), F32)

    for r0 in range(0, tm, row_chunk):
        rows = slice(r0, r0 + row_chunk)
        h = h_ref[rows, :]
        gate = jnp.dot(h, wgb_ref[...], preferred_element_type=F32)
        up = jnp.dot(h, wub_ref[...], preferred_element_type=F32)
        for lc in range(tn // LANES):
            lanes = slice(lc * LANES, (lc + 1) * LANES)
            g0 = gate[:, lanes]
            gbuf_ref[lc, SUBLANES + r0:SUBLANES + r0 + row_chunk, :] = g0
            g1 = gbuf_ref[lc, pl.ds(SUBLANES + r0 - 1, row_chunk), :]
            g2 = gbuf_ref[lc, pl.ds(SUBLANES + r0 - 2, row_chunk), :]
            conv = (cb_ref[:, lanes] + cw_ref[2:3, lanes] * g0 + cw_ref[1:2, lanes] * g1
                    + cw_ref[0:1, lanes] * g2)
            gelu = 0.5 * conv * (1.0 + lax.erf(conv * math.sqrt(0.5)))
            o_ref[rows, lanes] = (gelu * up[:, lanes]).astype(BF16)
    gbuf_ref[:, 0:SUBLANES, :] = gbuf_ref[:, tm:tm + SUBLANES, :]


def _ffnup(h2, w_up, conv_w, conv_b, seq, tm=2048, tn=512, row_chunk=256):
    t, d = h2.shape
    d_ff = w_up.shape[1] // 2
    assert t % tm == 0 and seq % tm == 0 and d_ff % tn == 0 and conv_w.shape[0] == FFN_CONV_K
    assert tm % row_chunk == 0 and tn % LANES == 0
    n_col = d_ff // tn
    return pl.pallas_call(
        functools.partial(_ffnup_kernel, tiles_per_seq=seq // tm, row_chunk=row_chunk),
        grid=(n_col, t // tm),
        in_specs=[
            pl.BlockSpec((tm, d), lambda j, i: (i, 0)),
            pl.BlockSpec((d, tn), lambda j, i: (0, j)),
            pl.BlockSpec((d, tn), lambda j, i: (0, n_col + j)),
            pl.BlockSpec((FFN_CONV_K, tn), lambda j, i: (0, j)),
            pl.BlockSpec((1, tn), lambda j, i: (0, j)),
        ],
        out_specs=pl.BlockSpec((tm, tn), lambda j, i: (i, j)),
        out_shape=jax.ShapeDtypeStruct((t, d_ff), BF16),
        scratch_shapes=[pltpu.VMEM((d, tn), BF16), pltpu.VMEM((d, tn), BF16),
                        pltpu.VMEM((tn // LANES, tm + SUBLANES, LANES), F32)],
        compiler_params=_params("parallel", "arbitrary"),
        name="ffnup",
    )(h2, w_up, w_up, conv_w, conv_b.reshape(1, d_ff))


def _ffndown_kernel(a_ref, w_ref, x1_ref, g_ref, o_ref, *, final_norm, row_chunk):
    for r0 in range(0, a_ref.shape[0], row_chunk):
        rows = slice(r0, r0 + row_chunk)
        x2 = x1_ref[rows, :] + jnp.dot(a_ref[rows, :], w_ref[...], preferred_element_type=F32)
        o_ref[rows, :] = _rmsnorm(x2, g_ref[...]) if final_norm else x2


def _ffndown(act, w_down, x1, g, final_norm, tm=512, row_chunk=256):
    t, d_ff = act.shape
    d = w_down.shape[1]
    assert t % tm == 0 and tm % row_chunk == 0
    return pl.pallas_call(
        functools.partial(_ffndown_kernel, final_norm=final_norm, row_chunk=row_chunk),
        grid=(t // tm,),
        in_specs=[
            pl.BlockSpec((tm, d_ff), lambda i: (i, 0)),
            _resident((d_ff, d), lambda i: (0, 0)),
            pl.BlockSpec((tm, d), lambda i: (i, 0)),
            pl.BlockSpec((1, d), lambda i: (0, 0)),
        ],
        out_specs=pl.BlockSpec((tm, d), lambda i: (i, 0)),
        out_shape=jax.ShapeDtypeStruct((t, d), F32),
        compiler_params=_params("parallel"),
        name="ffndown",
    )(act, w_down, x1, g.reshape(1, d))


def _layer(x2d, bsz, seq, layer_idx, is_last, norm_final_g, norm_mix_g, w_in, conv_dw_w,
           conv_dw_b, conv_ln_g, conv_ln_b, lambda_q1, lambda_k1, lambda_q2, lambda_k2, subln_g,
           w_out, norm_ffn_g, w_up, ffn_dw_w, ffn_dw_b, w_down):
    conv_ch = conv_dw_w.shape[1]
    lambda_init = 0.8 - 0.6 * math.exp(-0.3 * layer_idx)
    u, qkv = _inproj(x2d, norm_mix_g, w_in.astype(BF16), conv_ch)
    conv_out = _convbranch(u.reshape(bsz, seq, conv_ch), conv_dw_w, conv_dw_b, conv_ln_g,
                           conv_ln_b).reshape(bsz * seq, conv_ch)
    lam_params = jnp.stack([lambda_q1, lambda_k1, lambda_q2, lambda_k2]).astype(F32)
    attn_out = _attention(qkv, lam_params, subln_g, bsz, seq, lambda_init)
    x1, h2 = _outproj(conv_out, attn_out, x2d, w_out.astype(BF16), norm_ffn_g)
    act = _ffnup(h2, w_up, ffn_dw_w, ffn_dw_b, seq)
    final_g = norm_final_g if is_last else jnp.ones_like(norm_final_g)
    return _ffndown(act, w_down.astype(BF16), x1, final_g, final_norm=is_last)


def kernel(x, norm_mix_g, w_in, conv_dw_w, conv_dw_b, conv_ln_g, conv_ln_b, lambda_q1, lambda_k1,
           lambda_q2, lambda_k2, subln_g, w_out, norm_ffn_g, w_up, ffn_dw_w, ffn_dw_b, w_down,
           norm_final_g):
    bsz, seq, d = x.shape
    depth = w_in.shape[0]
    x2d = x.reshape(bsz * seq, d)
    for l in range(depth):
        x2d = _layer(x2d, bsz, seq, l, l == depth - 1, norm_final_g, norm_mix_g[l], w_in[l],
                     conv_dw_w[l], conv_dw_b[l], conv_ln_g[l], conv_ln_b[l], lambda_q1[l],
                     lambda_k1[l], lambda_q2[l], lambda_k2[l], subln_g[l], w_out[l],
                     norm_ffn_g[l], w_up[l], ffn_dw_w[l], ffn_dw_b[l], w_down[l])
    return x2d.reshape(bsz, seq, d)
```

```python
import functools
import math

import jax
import jax.numpy as jnp
from jax import lax
from jax.experimental import pallas as pl
from jax.experimental.pallas import tpu as pltpu

F32 = jnp.float32
BF16 = jnp.bfloat16

CHUNK = 64
CONV_K = 31
FFN_CONV_K = 3
ATT_HEADS = 8
ATT_QK_DIM = 64
HEAD_W = 2 * ATT_QK_DIM
NORM_EPS = 1e-6
LN_EPS = 1e-5
MASK_VALUE = -1e30
Q_PRESCALE = math.log2(math.e) / math.sqrt(ATT_QK_DIM)

VMEM_LIMIT_BYTES = 56 * 1024 * 1024
SUBLANES = 8
LANES = 128
BF16_SUBLANES = 16
CONV_HALO = 32


def _params(*semantics):
    return pltpu.CompilerParams(dimension_semantics=semantics,
                                vmem_limit_bytes=VMEM_LIMIT_BYTES)


def _resident(block_shape, index_map):
    return pl.BlockSpec(block_shape, index_map, pipeline_mode=pl.Buffered(1))


def _rmsnorm(x, g):
    ms = jnp.mean(x * x, axis=-1, keepdims=True)
    return x * lax.rsqrt(ms + NORM_EPS) * g


def _sigmoid(x):
    return 1.0 / (1.0 + jnp.exp(-x))


def _inproj_kernel(x_ref, g_ref, w_ref, side_ref, u_ref, qkv_ref, side_bf16_ref, *, row_chunk):
    side_bf16_ref[...] = side_ref[...].astype(BF16)
    tm, n = u_ref.shape
    n_qkv = qkv_ref.shape[1] // n
    for r0 in range(0, tm, row_chunk):
        rows = slice(r0, r0 + row_chunk)
        h = _rmsnorm(x_ref[rows, :], g_ref[...]).astype(BF16)
        a = jnp.dot(h, w_ref[:, 0:n], preferred_element_type=F32)
        gate = jnp.dot(h, w_ref[:, n:2 * n], preferred_element_type=F32)
        u_ref[rows, :] = a * _sigmoid(gate)
        for c in range(n_qkv):
            z = jnp.dot(h, w_ref[:, (2 + c) * n:(3 + c) * n], preferred_element_type=F32)
            if c == 0:
                z = z * Q_PRESCALE
            qkv_ref[rows, c * n:(c + 1) * n] = z.astype(BF16)


def _inproj(x2d, g, w_in, conv_ch, side_w, tm=512, row_chunk=256):
    t, d = x2d.shape
    n = conv_ch
    n_groups = w_in.shape[1] // n
    n_steps = t // tm
    side_rows = side_w.shape[0] // n_steps
    assert w_in.shape[1] == n_groups * n and t % tm == 0 and tm % row_chunk == 0
    assert side_rows * n_steps == side_w.shape[0] and side_rows % BF16_SUBLANES == 0
    return pl.pallas_call(
        functools.partial(_inproj_kernel, row_chunk=row_chunk),
        grid=(n_steps,),
        in_specs=[
            pl.BlockSpec((tm, d), lambda i: (i, 0)),
            pl.BlockSpec((1, d), lambda i: (0, 0)),
            _resident((d, n_groups * n), lambda i: (0, 0)),
            pl.BlockSpec((side_rows, side_w.shape[1]), lambda i: (i, 0)),
        ],
        out_specs=[
            pl.BlockSpec((tm, n), lambda i: (i, 0)),
            pl.BlockSpec((tm, (n_groups - 2) * n), lambda i: (i, 0)),
            pl.BlockSpec((side_rows, side_w.shape[1]), lambda i: (i, 0)),
        ],
        out_shape=[
            jax.ShapeDtypeStruct((t, n), F32),
            jax.ShapeDtypeStruct((t, (n_groups - 2) * n), BF16),
            jax.ShapeDtypeStruct(side_w.shape, BF16),
        ],
        compiler_params=_params("parallel"),
        name="inproj",
    )(x2d, g.reshape(1, d), w_in, side_w)


def _convbranch_kernel(u_ref, halo_ref, w_ref, b_ref, lg_ref, lb_ref, o_ref, win_ref, y_ref):
    t = pl.program_id(1)
    ts, c = u_ref.shape[1], u_ref.shape[2]
    n_chunks = c // LANES
    n_groups = ts // SUBLANES
    first_tap_row = CONV_HALO - (CONV_K - 1)

    for lc in range(n_chunks):
        lanes = slice(lc * LANES, (lc + 1) * LANES)
        halo = halo_ref[0, :, lanes]
        win_ref[lc, 0:CONV_HALO, :] = jnp.where(t > 0, halo, jnp.zeros_like(halo))
        win_ref[lc, CONV_HALO:, :] = u_ref[0, :, lanes]

    def conv_chunk(lc, carry):
        bias = jnp.broadcast_to(b_ref[lc], (SUBLANES, LANES))
        accs = [bias for _ in range(n_groups)]
        for b in range(SUBLANES):
            taps = list(range(b, CONV_K, SUBLANES))
            wks = [jnp.broadcast_to(w_ref[lc, k:k + 1, :], (SUBLANES, LANES)) for k in taps]
            for j in range(n_groups + len(taps) - 1):
                xw = win_ref[lc, pl.ds(first_tap_row + b + SUBLANES * j, SUBLANES), :]
                for a, wk in enumerate(wks):
                    g = j - a
                    if 0 <= g < n_groups:
                        accs[g] = accs[g] + xw * wk
        for g in range(n_groups):
            y_ref[lc, pl.ds(SUBLANES * g, SUBLANES), :] = accs[g]
        return carry

    lax.fori_loop(0, n_chunks, conv_chunk, 0)

    y = y_ref[...]
    mu = jnp.sum(jnp.sum(y, axis=0), axis=-1, keepdims=True) * (1.0 / c)
    yc = y - mu
    var = jnp.sum(jnp.sum(yc * yc, axis=0), axis=-1, keepdims=True) * (1.0 / c)
    z = yc * lax.rsqrt(var + LN_EPS) * lg_ref[...] + lb_ref[...]
    act = (z * _sigmoid(z)).astype(BF16)
    for lc in range(n_chunks):
        o_ref[0, :, lc * LANES:(lc + 1) * LANES] = act[lc]


def _convbranch(u, w, b, ln_g, ln_b, ts=256):
    bsz, seq, c = u.shape
    assert seq % ts == 0 and ts % CONV_HALO == 0 and c % LANES == 0 and w.shape[0] == CONV_K
    halo_blocks = ts // CONV_HALO
    n_chunks = c // LANES

    def per_chunk(p):
        return p.reshape(-1, n_chunks, LANES).transpose(1, 0, 2)

    def whole(rows):
        return pl.BlockSpec((n_chunks, rows, LANES), lambda bi, ti: (0, 0, 0))

    return pl.pallas_call(
        _convbranch_kernel,
        grid=(bsz, seq // ts),
        in_specs=[
            pl.BlockSpec((1, ts, c), lambda bi, ti: (bi, ti, 0)),
            pl.BlockSpec((1, CONV_HALO, c),
                         lambda bi, ti: (bi, jnp.maximum(ti * halo_blocks - 1, 0), 0)),
            whole(CONV_K), whole(1), whole(1), whole(1),
        ],
        out_specs=pl.BlockSpec((1, ts, c), lambda bi, ti: (bi, ti, 0)),
        out_shape=jax.ShapeDtypeStruct((bsz, seq, c), BF16),
        scratch_shapes=[pltpu.VMEM((n_chunks, ts + CONV_HALO, LANES), F32),
                        pltpu.VMEM((n_chunks, ts, LANES), F32)],
        compiler_params=_params("parallel", "parallel"),
        name="convbranch",
    )(u, u, per_chunk(w), per_chunk(b), per_chunk(ln_g), per_chunk(ln_b))


def _attn_kernel(lamp_ref, q_ref, k_ref, v_ref, sg_ref, o_ref, qz_ref, vt_ref, st_ref, smax_ref,
                 m_ref, acc_ref, *, lambda_init, n_ahead):
    qi = pl.program_id(2)
    tq = q_ref.shape[0]
    n_heads, n_blocks, vt_rows, tk = vt_ref.shape
    assert tk == tq

    def head_lanes(g):
        return slice(g * HEAD_W, (g + 1) * HEAD_W)

    @pl.when(qi == 0)
    def _():
        pad_row = lax.broadcasted_iota(jnp.int32, (vt_rows - HEAD_W, tk), 0)
        ones_then_zeros = jnp.where(pad_row == 0, 1.0, 0.0).astype(BF16)
        for g in range(n_heads):
            for j in range(n_blocks):
                vt_ref[g, j, 0:HEAD_W, :] = (
                    v_ref[j * tk:(j + 1) * tk, head_lanes(g)].astype(F32).T.astype(BF16))
                vt_ref[g, j, HEAD_W:, :] = ones_then_zeros

    for g in range(n_heads):
        q = q_ref[:, head_lanes(g)]
        lane = lax.broadcasted_iota(jnp.int32, q.shape, 1)
        zero = jnp.zeros_like(q)
        qz_ref[g, 0:tq, :] = jnp.where(lane < ATT_QK_DIM, q, zero)
        qz_ref[g, tq:, :] = jnp.where(lane >= ATT_QK_DIM, q, zero)
    m_ref[...] = jnp.full(m_ref.shape, -jnp.inf, F32)
    acc_ref[...] = jnp.zeros(acc_ref.shape, F32)

    def scores(j, g, diagonal):
        k0 = pl.multiple_of(j * tk, tk)
        kb = k_ref[pl.ds(k0, tk), head_lanes(g)]
        st = lax.dot_general(kb, qz_ref[g], (((1,), (1,)), ((), ())),
                             preferred_element_type=F32)
        if diagonal is not False:
            key = lax.broadcasted_iota(jnp.int32, st.shape, 0)
            col = lax.broadcasted_iota(jnp.int32, st.shape, 1)
            q_chunk = jnp.where(col >= tq, col - tq, col) // CHUNK
            allowed = key // CHUNK <= q_chunk
            if diagonal is None:
                allowed = jnp.logical_or(allowed, j != qi)
            st = jnp.where(allowed, st, MASK_VALUE)
        return st, jnp.max(st, axis=0, keepdims=True)

    def softmax_pv(j, g, st_and_max):
        st, m_blk = st_and_max
        m_prev = m_ref[g]
        m_new = jnp.maximum(m_prev, m_blk)
        alpha = jnp.exp2(m_prev - m_new)
        p = jnp.exp2(st - m_new)
        acc_ref[g] = alpha * acc_ref[g] + jnp.dot(vt_ref[g, j], p.astype(BF16),
                                                  preferred_element_type=F32)
        m_ref[g] = m_new

    n_slots = st_ref.shape[0]

    def key_block(j, diagonal):
        for g in range(n_heads):
            ahead = g + n_ahead
            if ahead < n_heads:
                st_ref[ahead % n_slots], smax_ref[ahead % n_slots] = scores(j, ahead, diagonal)
            elif not diagonal:
                st_ref[ahead % n_slots], smax_ref[ahead % n_slots] = scores(
                    j + 1, ahead - n_heads, None)
            softmax_pv(j, g, (st_ref[g % n_slots], smax_ref[g % n_slots]))

    def body(j, carry):
        key_block(j, diagonal=False)
        return carry

    for a in range(n_ahead):
        st_ref[a], smax_ref[a] = scores(0, a, None)
    lax.fori_loop(0, qi, body, 0)
    key_block(qi, diagonal=True)

    lp = lamp_ref[...]
    s1 = jnp.sum(lp[0:1] * lp[1:2], axis=-1, keepdims=True)
    s2 = jnp.sum(lp[2:3] * lp[3:4], axis=-1, keepdims=True)
    lam = jnp.exp(s1) - jnp.exp(s2) + lambda_init
    for g in range(n_heads):
        inv_l = 1.0 / acc_ref[g, HEAD_W:HEAD_W + 1, :]
        o = acc_ref[g, 0:HEAD_W, :] * inv_l
        o = o[:, 0:tq] - lam * o[:, tq:]
        ms = jnp.mean(o * o, axis=0, keepdims=True)
        y = (o * lax.rsqrt(ms + NORM_EPS)).T
        o_ref[:, head_lanes(g)] = (y * sg_ref[...] * (1.0 - lambda_init)).astype(BF16)


def _attention(qkv, lam_params, subln_g, bsz, seq, lambda_init, tq=256, heads_per_step=8,
               heads_ahead=2, score_slots=4):
    t = qkv.shape[0]
    n_q_tiles = seq // tq
    gw = heads_per_step * HEAD_W
    n_groups = ATT_HEADS // heads_per_step
    vt_rows = HEAD_W + BF16_SUBLANES
    assert t == bsz * seq and seq % tq == 0 and tq % CHUNK == 0 and ATT_HEADS % heads_per_step == 0
    assert heads_ahead < score_slots and heads_per_step % score_slots == 0
    return pl.pallas_call(
        functools.partial(_attn_kernel, lambda_init=lambda_init, n_ahead=heads_ahead),
        grid=(bsz, n_groups, n_q_tiles),
        in_specs=[
            pl.BlockSpec((4, ATT_QK_DIM), lambda b, hg, qi: (0, 0)),
            pl.BlockSpec((tq, gw), lambda b, hg, qi: (b * n_q_tiles + qi, hg)),
            pl.BlockSpec((seq, gw), lambda b, hg, qi: (b, n_groups + hg)),
            pl.BlockSpec((seq, gw), lambda b, hg, qi: (b, 2 * n_groups + hg)),
            pl.BlockSpec((1, HEAD_W), lambda b, hg, qi: (0, 0)),
        ],
        out_specs=pl.BlockSpec((tq, gw), lambda b, hg, qi: (b * n_q_tiles + qi, hg)),
        out_shape=jax.ShapeDtypeStruct((t, ATT_HEADS * HEAD_W), BF16),
        scratch_shapes=[
            pltpu.VMEM((heads_per_step, 2 * tq, HEAD_W), BF16),
            pltpu.VMEM((heads_per_step, seq // tq, vt_rows, tq), BF16),
            pltpu.VMEM((score_slots, tq, 2 * tq), F32),
            pltpu.VMEM((score_slots, 1, 2 * tq), F32),
            pltpu.VMEM((heads_per_step, 1, 2 * tq), F32),
            pltpu.VMEM((heads_per_step, vt_rows, 2 * tq), F32),
        ],
        compiler_params=_params("parallel", "parallel", "arbitrary"),
        name="diffattn",
    )(lam_params, qkv, qkv, qkv, subln_g.reshape(1, HEAD_W))


def _outproj_kernel(c_ref, a_ref, x_ref, wc_ref, wa_ref, g_ref, x1_ref, h_ref, *, row_chunk):
    for r0 in range(0, x_ref.shape[0], row_chunk):
        rows = slice(r0, r0 + row_chunk)
        mix = jnp.dot(c_ref[rows, :], wc_ref[...], preferred_element_type=F32)
        mix = mix + jnp.dot(a_ref[rows, :], wa_ref[...], preferred_element_type=F32)
        x1 = x_ref[rows, :] + mix
        x1_ref[rows, :] = x1
        h_ref[rows, :] = _rmsnorm(x1, g_ref[...]).astype(BF16)


def _outproj(conv_out, attn_out, x2d, w_out, g, tm=512, row_chunk=256):
    t, d = x2d.shape
    c = conv_out.shape[1]
    a = attn_out.shape[1]
    assert w_out.shape == (c + a, d) and c == a and t % tm == 0 and tm % row_chunk == 0
    return pl.pallas_call(
        functools.partial(_outproj_kernel, row_chunk=row_chunk),
        grid=(t // tm,),
        in_specs=[
            pl.BlockSpec((tm, c), lambda i: (i, 0)),
            pl.BlockSpec((tm, a), lambda i: (i, 0)),
            pl.BlockSpec((tm, d), lambda i: (i, 0)),
            _resident((c, d), lambda i: (0, 0)),
            _resident((a, d), lambda i: (1, 0)),
            pl.BlockSpec((1, d), lambda i: (0, 0)),
        ],
        out_specs=[pl.BlockSpec((tm, d), lambda i: (i, 0)),
                   pl.BlockSpec((tm, d), lambda i: (i, 0))],
        out_shape=[jax.ShapeDtypeStruct((t, d), F32), jax.ShapeDtypeStruct((t, d), BF16)],
        compiler_params=_params("parallel"),
        name="outproj",
    )(conv_out, attn_out, x2d, w_out, w_out, g.reshape(1, d))


def _ffnup_kernel(h_ref, wg_ref, wu_ref, cw_ref, cb_ref, side_ref, o_ref, side_bf16_ref, wgb_ref,
                  wub_ref, gbuf_ref, *, tiles_per_seq, row_chunk):
    i = pl.program_id(1)
    tm, tn = o_ref.shape
    side_bf16_ref[...] = side_ref[...].astype(BF16)

    @pl.when(i == 0)
    def _():
        wgb_ref[...] = wg_ref[...].astype(BF16)
        wub_ref[...] = wu_ref[...].astype(BF16)

    @pl.when(i % tiles_per_seq == 0)
    def _():
        gbuf_ref[:, 0:SUBLANES, :] = jnp.zeros((tn // LANES, SUBLANES, LANES), F32)

    for r0 in range(0, tm, row_chunk):
        rows = slice(r0, r0 + row_chunk)
        h = h_ref[rows, :]
        gate = jnp.dot(h, wgb_ref[...], preferred_element_type=F32)
        up = jnp.dot(h, wub_ref[...], preferred_element_type=F32)
        for lc in range(tn // LANES):
            lanes = slice(lc * LANES, (lc + 1) * LANES)
            g0 = gate[:, lanes]
            gbuf_ref[lc, SUBLANES + r0:SUBLANES + r0 + row_chunk, :] = g0
            g1 = gbuf_ref[lc, pl.ds(SUBLANES + r0 - 1, row_chunk), :]
            g2 = gbuf_ref[lc, pl.ds(SUBLANES + r0 - 2, row_chunk), :]
            conv = (cb_ref[:, lanes] + cw_ref[2:3, lanes] * g0 + cw_ref[1:2, lanes] * g1
                    + cw_ref[0:1, lanes] * g2)
            gelu = 0.5 * conv * (1.0 + lax.erf(conv * math.sqrt(0.5)))
            o_ref[rows, lanes] = (gelu * up[:, lanes]).astype(BF16)
    gbuf_ref[:, 0:SUBLANES, :] = gbuf_ref[:, tm:tm + SUBLANES, :]


def _ffnup(h2, w_up, conv_w, conv_b, seq, side_w, tm=2048, tn=512, row_chunk=256):
    t, d = h2.shape
    d_ff = w_up.shape[1] // 2
    assert t % tm == 0 and seq % tm == 0 and d_ff % tn == 0 and conv_w.shape[0] == FFN_CONV_K
    assert tm % row_chunk == 0 and tn % LANES == 0
    n_col = d_ff // tn
    n_tok = t // tm
    side_rows = side_w.shape[0] // (n_col * n_tok)
    assert side_rows * n_col * n_tok == side_w.shape[0] and side_rows % BF16_SUBLANES == 0
    side_spec = pl.BlockSpec((side_rows, side_w.shape[1]), lambda j, i: (j * n_tok + i, 0))
    return pl.pallas_call(
        functools.partial(_ffnup_kernel, tiles_per_seq=seq // tm, row_chunk=row_chunk),
        grid=(n_col, n_tok),
        in_specs=[
            pl.BlockSpec((tm, d), lambda j, i: (i, 0)),
            pl.BlockSpec((d, tn), lambda j, i: (0, j)),
            pl.BlockSpec((d, tn), lambda j, i: (0, n_col + j)),
            pl.BlockSpec((FFN_CONV_K, tn), lambda j, i: (0, j)),
            pl.BlockSpec((1, tn), lambda j, i: (0, j)),
            side_spec,
        ],
        out_specs=[pl.BlockSpec((tm, tn), lambda j, i: (i, j)), side_spec],
        out_shape=[jax.ShapeDtypeStruct((t, d_ff), BF16),
                   jax.ShapeDtypeStruct(side_w.shape, BF16)],
        scratch_shapes=[pltpu.VMEM((d, tn), BF16), pltpu.VMEM((d, tn), BF16),
                        pltpu.VMEM((tn // LANES, tm + SUBLANES, LANES), F32)],
        compiler_params=_params("parallel", "arbitrary"),
        name="ffnup",
    )(h2, w_up, w_up, conv_w, conv_b.reshape(1, d_ff), side_w)


def _ffndown_kernel(a_ref, w_ref, x1_ref, g_ref, o_ref, *, final_norm, row_chunk):
    for r0 in range(0, a_ref.shape[0], row_chunk):
        rows = slice(r0, r0 + row_chunk)
        x2 = x1_ref[rows, :] + jnp.dot(a_ref[rows, :], w_ref[...], preferred_element_type=F32)
        o_ref[rows, :] = _rmsnorm(x2, g_ref[...]) if final_norm else x2


def _ffndown(act, w_down, x1, g, final_norm, tm=512, row_chunk=256):
    t, d_ff = act.shape
    d = w_down.shape[1]
    assert t % tm == 0 and tm % row_chunk == 0
    return pl.pallas_call(
        functools.partial(_ffndown_kernel, final_norm=final_norm, row_chunk=row_chunk),
        grid=(t // tm,),
        in_specs=[
            pl.BlockSpec((tm, d_ff), lambda i: (i, 0)),
            _resident((d_ff, d), lambda i: (0, 0)),
            pl.BlockSpec((tm, d), lambda i: (i, 0)),
            pl.BlockSpec((1, d), lambda i: (0, 0)),
        ],
        out_specs=pl.BlockSpec((tm, d), lambda i: (i, 0)),
        out_shape=jax.ShapeDtypeStruct((t, d), F32),
        compiler_params=_params("parallel"),
        name="ffndown",
    )(act, w_down, x1, g.reshape(1, d))


def _layer(x2d, bsz, seq, layer_idx, is_last, norm_final_g, norm_mix_g, w_in, conv_dw_w,
           conv_dw_b, conv_ln_g, conv_ln_b, lambda_q1, lambda_k1, lambda_q2, lambda_k2, subln_g,
           w_out, norm_ffn_g, w_up, ffn_dw_w, ffn_dw_b, w_down):
    conv_ch = conv_dw_w.shape[1]
    lambda_init = 0.8 - 0.6 * math.exp(-0.3 * layer_idx)
    u, qkv, w_out_bf16 = _inproj(x2d, norm_mix_g, w_in.astype(BF16), conv_ch, w_out)
    conv_out = _convbranch(u.reshape(bsz, seq, conv_ch), conv_dw_w, conv_dw_b, conv_ln_g,
                           conv_ln_b).reshape(bsz * seq, conv_ch)
    lam_params = jnp.stack([lambda_q1, lambda_k1, lambda_q2, lambda_k2]).astype(F32)
    attn_out = _attention(qkv, lam_params, subln_g, bsz, seq, lambda_init)
    x1, h2 = _outproj(conv_out, attn_out, x2d, w_out_bf16, norm_ffn_g)
    act, w_down_bf16 = _ffnup(h2, w_up, ffn_dw_w, ffn_dw_b, seq, w_down)
    final_g = norm_final_g if is_last else jnp.ones_like(norm_final_g)
    return _ffndown(act, w_down_bf16, x1, final_g, final_norm=is_last)


def kernel(x, norm_mix_g, w_in, conv_dw_w, conv_dw_b, conv_ln_g, conv_ln_b, lambda_q1, lambda_k1,
           lambda_q2, lambda_k2, subln_g, w_out, norm_ffn_g, w_up, ffn_dw_w, ffn_dw_b, w_down,
           norm_final_g):
    bsz, seq, d = x.shape
    depth = w_in.shape[0]
    x2d = x.reshape(bsz * seq, d)
    for l in range(depth):
        x2d = _layer(x2d, bsz, seq, l, l == depth - 1, norm_final_g, norm_mix_g[l], w_in[l],
                     conv_dw_w[l], conv_dw_b[l], conv_ln_g[l], conv_ln_b[l], lambda_q1[l],
                     lambda_k1[l], lambda_q2[l], lambda_k2[l], subln_g[l], w_out[l],
                     norm_ffn_g[l], w_up[l], ffn_dw_w[l], ffn_dw_b[l], w_down[l])
    return x2d.reshape(bsz, seq, d)
```

```python
import functools
import math

import jax
import jax.numpy as jnp
from jax import lax
from jax.experimental import pallas as pl
from jax.experimental.pallas import tpu as pltpu

F32 = jnp.float32
BF16 = jnp.bfloat16

CHUNK = 64
CONV_K = 31
FFN_CONV_K = 3
ATT_HEADS = 8
ATT_QK_DIM = 64
HEAD_W = 2 * ATT_QK_DIM
NORM_EPS = 1e-6
LN_EPS = 1e-5
MASK_VALUE = -1e30
Q_PRESCALE = math.log2(math.e) / math.sqrt(ATT_QK_DIM)

VMEM_LIMIT_BYTES = 56 * 1024 * 1024
SUBLANES = 8
LANES = 128
BF16_SUBLANES = 16
CONV_HALO = 32


def _params(*semantics):
    return pltpu.CompilerParams(dimension_semantics=semantics,
                                vmem_limit_bytes=VMEM_LIMIT_BYTES)


def _resident(block_shape, index_map):
    return pl.BlockSpec(block_shape, index_map, pipeline_mode=pl.Buffered(1))


def _rmsnorm(x, g):
    ms = jnp.mean(x * x, axis=-1, keepdims=True)
    return x * lax.rsqrt(ms + NORM_EPS) * g


def _sigmoid(x):
    return 1.0 / (1.0 + jnp.exp(-x))


def _inproj_kernel(x_ref, g_ref, w_ref, side_ref, u_ref, qkv_ref, side_bf16_ref, *, row_chunk):
    side_bf16_ref[...] = side_ref[...].astype(BF16)
    tm, n = u_ref.shape
    n_qkv = qkv_ref.shape[1] // n
    for r0 in range(0, tm, row_chunk):
        rows = slice(r0, r0 + row_chunk)
        h = _rmsnorm(x_ref[rows, :], g_ref[...]).astype(BF16)
        a = jnp.dot(h, w_ref[:, 0:n], preferred_element_type=F32)
        gate = jnp.dot(h, w_ref[:, n:2 * n], preferred_element_type=F32)
        u_ref[rows, :] = a * _sigmoid(gate)
        for c in range(n_qkv):
            z = jnp.dot(h, w_ref[:, (2 + c) * n:(3 + c) * n], preferred_element_type=F32)
            if c == 0:
                z = z * Q_PRESCALE
            qkv_ref[rows, c * n:(c + 1) * n] = z.astype(BF16)


def _inproj(x2d, g, w_in, conv_ch, side_w, tm=512, row_chunk=256):
    t, d = x2d.shape
    n = conv_ch
    n_groups = w_in.shape[1] // n
    n_steps = t // tm
    side_rows = side_w.shape[0] // n_steps
    assert w_in.shape[1] == n_groups * n and t % tm == 0 and tm % row_chunk == 0
    assert side_rows * n_steps == side_w.shape[0] and side_rows % BF16_SUBLANES == 0
    return pl.pallas_call(
        functools.partial(_inproj_kernel, row_chunk=row_chunk),
        grid=(n_steps,),
        in_specs=[
            pl.BlockSpec((tm, d), lambda i: (i, 0)),
            pl.BlockSpec((1, d), lambda i: (0, 0)),
            _resident((d, n_groups * n), lambda i: (0, 0)),
            pl.BlockSpec((side_rows, side_w.shape[1]), lambda i: (i, 0)),
        ],
        out_specs=[
            pl.BlockSpec((tm, n), lambda i: (i, 0)),
            pl.BlockSpec((tm, (n_groups - 2) * n), lambda i: (i, 0)),
            pl.BlockSpec((side_rows, side_w.shape[1]), lambda i: (i, 0)),
        ],
        out_shape=[
            jax.ShapeDtypeStruct((t, n), F32),
            jax.ShapeDtypeStruct((t, (n_groups - 2) * n), BF16),
            jax.ShapeDtypeStruct(side_w.shape, BF16),
        ],
        compiler_params=_params("parallel"),
        name="inproj",
    )(x2d, g.reshape(1, d), w_in, side_w)


def _convbranch_kernel(u_ref, halo_ref, w_ref, b_ref, lg_ref, lb_ref, o_ref, win_ref, y_ref, *,
                       row_block):
    t = pl.program_id(1)
    ts, c = u_ref.shape[1], u_ref.shape[2]
    n_chunks = c // LANES
    n_groups = row_block // SUBLANES
    first_tap_row = CONV_HALO - (CONV_K - 1)

    for lc in range(n_chunks):
        lanes = slice(lc * LANES, (lc + 1) * LANES)
        halo = halo_ref[0, :, lanes]
        win_ref[lc, 0:CONV_HALO, :] = jnp.where(t > 0, halo, jnp.zeros_like(halo))
        win_ref[lc, CONV_HALO:, :] = u_ref[0, :, lanes]

    def conv_chunk(lc, carry):
        bias = jnp.broadcast_to(b_ref[lc], (SUBLANES, LANES))
        for r0 in range(0, ts, row_block):
            accs = [bias for _ in range(n_groups)]
            for b in range(SUBLANES):
                taps = list(range(b, CONV_K, SUBLANES))
                wks = [jnp.broadcast_to(w_ref[lc, k:k + 1, :], (SUBLANES, LANES)) for k in taps]
                for j in range(n_groups + len(taps) - 1):
                    row = r0 + first_tap_row + b + SUBLANES * j
                    xw = win_ref[lc, pl.ds(row, SUBLANES), :]
                    for a, wk in enumerate(wks):
                        g = j - a
                        if 0 <= g < n_groups:
                            accs[g] = accs[g] + xw * wk
            for g in range(n_groups):
                y_ref[lc, pl.ds(r0 + SUBLANES * g, SUBLANES), :] = accs[g]
        return carry

    lax.fori_loop(0, n_chunks, conv_chunk, 0)

    y = y_ref[...]
    mu = jnp.sum(jnp.sum(y, axis=0), axis=-1, keepdims=True) * (1.0 / c)
    yc = y - mu
    var = jnp.sum(jnp.sum(yc * yc, axis=0), axis=-1, keepdims=True) * (1.0 / c)
    z = yc * lax.rsqrt(var + LN_EPS) * lg_ref[...] + lb_ref[...]
    act = (z * _sigmoid(z)).astype(BF16)
    for lc in range(n_chunks):
        o_ref[0, :, lc * LANES:(lc + 1) * LANES] = act[lc]


def _convbranch(u, w, b, ln_g, ln_b, ts=1024, row_block=256):
    bsz, seq, c = u.shape
    assert seq % ts == 0 and ts % CONV_HALO == 0 and c % LANES == 0 and w.shape[0] == CONV_K
    assert ts % row_block == 0
    halo_blocks = ts // CONV_HALO
    n_chunks = c // LANES

    def per_chunk(p):
        return p.reshape(-1, n_chunks, LANES).transpose(1, 0, 2)

    def whole(rows):
        return pl.BlockSpec((n_chunks, rows, LANES), lambda bi, ti: (0, 0, 0))

    return pl.pallas_call(
        functools.partial(_convbranch_kernel, row_block=row_block),
        grid=(bsz, seq // ts),
        in_specs=[
            pl.BlockSpec((1, ts, c), lambda bi, ti: (bi, ti, 0)),
            pl.BlockSpec((1, CONV_HALO, c),
                         lambda bi, ti: (bi, jnp.maximum(ti * halo_blocks - 1, 0), 0)),
            whole(CONV_K), whole(1), whole(1), whole(1),
        ],
        out_specs=pl.BlockSpec((1, ts, c), lambda bi, ti: (bi, ti, 0)),
        out_shape=jax.ShapeDtypeStruct((bsz, seq, c), BF16),
        scratch_shapes=[pltpu.VMEM((n_chunks, ts + CONV_HALO, LANES), F32),
                        pltpu.VMEM((n_chunks, ts, LANES), F32)],
        compiler_params=_params("parallel", "parallel"),
        name="convbranch",
    )(u, u, per_chunk(w), per_chunk(b), per_chunk(ln_g), per_chunk(ln_b))


def _attn_kernel(lamp_ref, q_ref, k_ref, v_ref, sg_ref, o_ref, qz_ref, vt_ref, st_ref, smax_ref,
                 m_ref, acc_ref, *, lambda_init, n_ahead):
    qi = pl.program_id(2)
    tq = q_ref.shape[0]
    n_heads, n_blocks, vt_rows, tk = vt_ref.shape
    assert tk == tq

    def head_lanes(g):
        return slice(g * HEAD_W, (g + 1) * HEAD_W)

    @pl.when(qi == 0)
    def _():
        pad_row = lax.broadcasted_iota(jnp.int32, (vt_rows - HEAD_W, tk), 0)
        ones_then_zeros = jnp.where(pad_row == 0, 1.0, 0.0).astype(BF16)
        for g in range(n_heads):
            for j in range(n_blocks):
                vt_ref[g, j, 0:HEAD_W, :] = (
                    v_ref[j * tk:(j + 1) * tk, head_lanes(g)].astype(F32).T.astype(BF16))
                vt_ref[g, j, HEAD_W:, :] = ones_then_zeros

    for g in range(n_heads):
        q = q_ref[:, head_lanes(g)]
        lane = lax.broadcasted_iota(jnp.int32, q.shape, 1)
        zero = jnp.zeros_like(q)
        qz_ref[g, 0:tq, :] = jnp.where(lane < ATT_QK_DIM, q, zero)
        qz_ref[g, tq:, :] = jnp.where(lane >= ATT_QK_DIM, q, zero)
    m_ref[...] = jnp.full(m_ref.shape, -jnp.inf, F32)
    acc_ref[...] = jnp.zeros(acc_ref.shape, F32)

    def scores(j, g, diagonal):
        k0 = pl.multiple_of(j * tk, tk)
        kb = k_ref[pl.ds(k0, tk), head_lanes(g)]
        st = lax.dot_general(kb, qz_ref[g], (((1,), (1,)), ((), ())),
                             preferred_element_type=F32)
        if diagonal is not False:
            key = lax.broadcasted_iota(jnp.int32, st.shape, 0)
            col = lax.broadcasted_iota(jnp.int32, st.shape, 1)
            q_chunk = jnp.where(col >= tq, col - tq, col) // CHUNK
            allowed = key // CHUNK <= q_chunk
            if diagonal is None:
                allowed = jnp.logical_or(allowed, j != qi)
            st = jnp.where(allowed, st, MASK_VALUE)
        return st, jnp.max(st, axis=0, keepdims=True)

    def softmax_pv(j, g, st_and_max):
        st, m_blk = st_and_max
        m_prev = m_ref[g]
        m_new = jnp.maximum(m_prev, m_blk)
        alpha = jnp.exp2(m_prev - m_new)
        p = jnp.exp2(st - m_new)
        acc_ref[g] = alpha * acc_ref[g] + jnp.dot(vt_ref[g, j], p.astype(BF16),
                                                  preferred_element_type=F32)
        m_ref[g] = m_new

    n_slots = st_ref.shape[0]

    def key_block(j, diagonal):
        for g in range(n_heads):
            ahead = g + n_ahead
            if ahead < n_heads:
                st_ref[ahead % n_slots], smax_ref[ahead % n_slots] = scores(j, ahead, diagonal)
            elif not diagonal:
                st_ref[ahead % n_slots], smax_ref[ahead % n_slots] = scores(
                    j + 1, ahead - n_heads, None)
            softmax_pv(j, g, (st_ref[g % n_slots], smax_ref[g % n_slots]))

    def body(j, carry):
        key_block(j, diagonal=False)
        return carry

    for a in range(n_ahead):
        st_ref[a], smax_ref[a] = scores(0, a, None)
    lax.fori_loop(0, qi, body, 0)
    key_block(qi, diagonal=True)

    lp = lamp_ref[...]
    s1 = jnp.sum(lp[0:1] * lp[1:2], axis=-1, keepdims=True)
    s2 = jnp.sum(lp[2:3] * lp[3:4], axis=-1, keepdims=True)
    lam = jnp.exp(s1) - jnp.exp(s2) + lambda_init
    for g in range(n_heads):
        inv_l = 1.0 / acc_ref[g, HEAD_W:HEAD_W + 1, :]
        o = acc_ref[g, 0:HEAD_W, :] * inv_l
        o = o[:, 0:tq] - lam * o[:, tq:]
        ms = jnp.mean(o * o, axis=0, keepdims=True)
        y = (o * lax.rsqrt(ms + NORM_EPS)).T
        o_ref[:, head_lanes(g)] = (y * sg_ref[...] * (1.0 - lambda_init)).astype(BF16)


def _attention(qkv, lam_params, subln_g, bsz, seq, lambda_init, tq=256, heads_per_step=8,
               heads_ahead=2, score_slots=4):
    t = qkv.shape[0]
    n_q_tiles = seq // tq
    gw = heads_per_step * HEAD_W
    n_groups = ATT_HEADS // heads_per_step
    vt_rows = HEAD_W + BF16_SUBLANES
    assert t == bsz * seq and seq % tq == 0 and tq % CHUNK == 0 and ATT_HEADS % heads_per_step == 0
    assert heads_ahead < score_slots and heads_per_step % score_slots == 0
    return pl.pallas_call(
        functools.partial(_attn_kernel, lambda_init=lambda_init, n_ahead=heads_ahead),
        grid=(bsz, n_groups, n_q_tiles),
        in_specs=[
            pl.BlockSpec((4, ATT_QK_DIM), lambda b, hg, qi: (0, 0)),
            pl.BlockSpec((tq, gw), lambda b, hg, qi: (b * n_q_tiles + qi, hg)),
            pl.BlockSpec((seq, gw), lambda b, hg, qi: (b, n_groups + hg)),
            pl.BlockSpec((seq, gw), lambda b, hg, qi: (b, 2 * n_groups + hg)),
            pl.BlockSpec((1, HEAD_W), lambda b, hg, qi: (0, 0)),
        ],
        out_specs=pl.BlockSpec((tq, gw), lambda b, hg, qi: (b * n_q_tiles + qi, hg)),
        out_shape=jax.ShapeDtypeStruct((t, ATT_HEADS * HEAD_W), BF16),
        scratch_shapes=[
            pltpu.VMEM((heads_per_step, 2 * tq, HEAD_W), BF16),
            pltpu.VMEM((heads_per_step, seq // tq, vt_rows, tq), BF16),
            pltpu.VMEM((score_slots, tq, 2 * tq), F32),
            pltpu.VMEM((score_slots, 1, 2 * tq), F32),
            pltpu.VMEM((heads_per_step, 1, 2 * tq), F32),
            pltpu.VMEM((heads_per_step, vt_rows, 2 * tq), F32),
        ],
        compiler_params=_params("parallel", "parallel", "arbitrary"),
        name="diffattn",
    )(lam_params, qkv, qkv, qkv, subln_g.reshape(1, HEAD_W))


def _outproj_kernel(c_ref, a_ref, x_ref, wc_ref, wa_ref, g_ref, x1_ref, h_ref, *, row_chunk):
    for r0 in range(0, x_ref.shape[0], row_chunk):
        rows = slice(r0, r0 + row_chunk)
        mix = jnp.dot(c_ref[rows, :], wc_ref[...], preferred_element_type=F32)
        mix = mix + jnp.dot(a_ref[rows, :], wa_ref[...], preferred_element_type=F32)
        x1 = x_ref[rows, :] + mix
        x1_ref[rows, :] = x1
        h_ref[rows, :] = _rmsnorm(x1, g_ref[...]).astype(BF16)


def _outproj(conv_out, attn_out, x2d, w_out, g, tm=512, row_chunk=256):
    t, d = x2d.shape
    c = conv_out.shape[1]
    a = attn_out.shape[1]
    assert w_out.shape == (c + a, d) and c == a and t % tm == 0 and tm % row_chunk == 0
    return pl.pallas_call(
        functools.partial(_outproj_kernel, row_chunk=row_chunk),
        grid=(t // tm,),
        in_specs=[
            pl.BlockSpec((tm, c), lambda i: (i, 0)),
            pl.BlockSpec((tm, a), lambda i: (i, 0)),
            pl.BlockSpec((tm, d), lambda i: (i, 0)),
            _resident((c, d), lambda i: (0, 0)),
            _resident((a, d), lambda i: (1, 0)),
            pl.BlockSpec((1, d), lambda i: (0, 0)),
        ],
        out_specs=[pl.BlockSpec((tm, d), lambda i: (i, 0)),
                   pl.BlockSpec((tm, d), lambda i: (i, 0))],
        out_shape=[jax.ShapeDtypeStruct((t, d), F32), jax.ShapeDtypeStruct((t, d), BF16)],
        compiler_params=_params("parallel"),
        name="outproj",
    )(conv_out, attn_out, x2d, w_out, w_out, g.reshape(1, d))


def _ffnup_kernel(h_ref, wg_ref, wu_ref, cw_ref, cb_ref, side_ref, o_ref, side_bf16_ref, wgb_ref,
                  wub_ref, gbuf_ref, *, tiles_per_seq, row_chunk):
    i = pl.program_id(1)
    tm, tn = o_ref.shape
    side_bf16_ref[...] = side_ref[...].astype(BF16)

    @pl.when(i == 0)
    def _():
        wgb_ref[...] = wg_ref[...].astype(BF16)
        wub_ref[...] = wu_ref[...].astype(BF16)

    @pl.when(i % tiles_per_seq == 0)
    def _():
        gbuf_ref[:, 0:SUBLANES, :] = jnp.zeros((tn // LANES, SUBLANES, LANES), F32)

    for r0 in range(0, tm, row_chunk):
        rows = slice(r0, r0 + row_chunk)
        h = h_ref[rows, :]
        gate = jnp.dot(h, wgb_ref[...], preferred_element_type=F32)
        up = jnp.dot(h, wub_ref[...], preferred_element_type=F32)
        for lc in range(tn // LANES):
            lanes = slice(lc * LANES, (lc + 1) * LANES)
            g0 = gate[:, lanes]
            gbuf_ref[lc, SUBLANES + r0:SUBLANES + r0 + row_chunk, :] = g0
            g1 = gbuf_ref[lc, pl.ds(SUBLANES + r0 - 1, row_chunk), :]
            g2 = gbuf_ref[lc, pl.ds(SUBLANES + r0 - 2, row_chunk), :]
            conv = (cb_ref[:, lanes] + cw_ref[2:3, lanes] * g0 + cw_ref[1:2, lanes] * g1
                    + cw_ref[0:1, lanes] * g2)
            gelu = 0.5 * conv * (1.0 + lax.erf(conv * math.sqrt(0.5)))
            o_ref[rows, lanes] = (gelu * up[:, lanes]).astype(BF16)
    gbuf_ref[:, 0:SUBLANES, :] = gbuf_ref[:, tm:tm + SUBLANES, :]


def _ffnup(h2, w_up, conv_w, conv_b, seq, side_w, tm=2048, tn=512, row_chunk=256):
    t, d = h2.shape
    d_ff = w_up.shape[1] // 2
    assert t % tm == 0 and seq % tm == 0 and d_ff % tn == 0 and conv_w.shape[0] == FFN_CONV_K
    assert tm % row_chunk == 0 and tn % LANES == 0
    n_col = d_ff // tn
    n_tok = t // tm
    side_rows = side_w.shape[0] // (n_col * n_tok)
    assert side_rows * n_col * n_tok == side_w.shape[0] and side_rows % BF16_SUBLANES == 0
    side_spec = pl.BlockSpec((side_rows, side_w.shape[1]), lambda j, i: (j * n_tok + i, 0))
    return pl.pallas_call(
        functools.partial(_ffnup_kernel, tiles_per_seq=seq // tm, row_chunk=row_chunk),
        grid=(n_col, n_tok),
        in_specs=[
            pl.BlockSpec((tm, d), lambda j, i: (i, 0)),
            pl.BlockSpec((d, tn), lambda j, i: (0, j)),
            pl.BlockSpec((d, tn), lambda j, i: (0, n_col + j)),
            pl.BlockSpec((FFN_CONV_K, tn), lambda j, i: (0, j)),
            pl.BlockSpec((1, tn), lambda j, i: (0, j)),
            side_spec,
        ],
        out_specs=[pl.BlockSpec((tm, tn), lambda j, i: (i, j)), side_spec],
        out_shape=[jax.ShapeDtypeStruct((t, d_ff), BF16),
                   jax.ShapeDtypeStruct(side_w.shape, BF16)],
        scratch_shapes=[pltpu.VMEM((d, tn), BF16), pltpu.VMEM((d, tn), BF16),
                        pltpu.VMEM((tn // LANES, tm + SUBLANES, LANES), F32)],
        compiler_params=_params("parallel", "arbitrary"),
        name="ffnup",
    )(h2, w_up, w_up, conv_w, conv_b.reshape(1, d_ff), side_w)


def _ffndown_kernel(a_ref, w_ref, x1_ref, g_ref, o_ref, *, final_norm, row_chunk):
    for r0 in range(0, a_ref.shape[0], row_chunk):
        rows = slice(r0, r0 + row_chunk)
        x2 = x1_ref[rows, :] + jnp.dot(a_ref[rows, :], w_ref[...], preferred_element_type=F32)
        o_ref[rows, :] = _rmsnorm(x2, g_ref[...]) if final_norm else x2


def _ffndown(act, w_down, x1, g, final_norm, tm=512, row_chunk=256):
    t, d_ff = act.shape
    d = w_down.shape[1]
    assert t % tm == 0 and tm % row_chunk == 0
    return pl.pallas_call(
        functools.partial(_ffndown_kernel, final_norm=final_norm, row_chunk=row_chunk),
        grid=(t // tm,),
        in_specs=[
            pl.BlockSpec((tm, d_ff), lambda i: (i, 0)),
            _resident((d_ff, d), lambda i: (0, 0)),
            pl.BlockSpec((tm, d), lambda i: (i, 0)),
            pl.BlockSpec((1, d), lambda i: (0, 0)),
        ],
        out_specs=pl.BlockSpec((tm, d), lambda i: (i, 0)),
        out_shape=jax.ShapeDtypeStruct((t, d), F32),
        compiler_params=_params("parallel"),
        name="ffndown",
    )(act, w_down, x1, g.reshape(1, d))


def _layer(x2d, bsz, seq, layer_idx, is_last, norm_final_g, norm_mix_g, w_in, conv_dw_w,
           conv_dw_b, conv_ln_g, conv_ln_b, lambda_q1, lambda_k1, lambda_q2, lambda_k2, subln_g,
           w_out, norm_ffn_g, w_up, ffn_dw_w, ffn_dw_b, w_down):
    conv_ch = conv_dw_w.shape[1]
    lambda_init = 0.8 - 0.6 * math.exp(-0.3 * layer_idx)
    u, qkv, w_out_bf16 = _inproj(x2d, norm_mix_g, w_in.astype(BF16), conv_ch, w_out)
    conv_out = _convbranch(u.reshape(bsz, seq, conv_ch), conv_dw_w, conv_dw_b, conv_ln_g,
                           conv_ln_b).reshape(bsz * seq, conv_ch)
    lam_params = jnp.stack([lambda_q1, lambda_k1, lambda_q2, lambda_k2]).astype(F32)
    attn_out = _attention(qkv, lam_params, subln_g, bsz, seq, lambda_init)
    x1, h2 = _outproj(conv_out, attn_out, x2d, w_out_bf16, norm_ffn_g)
    act, w_down_bf16 = _ffnup(h2, w_up, ffn_dw_w, ffn_dw_b, seq, w_down)
    final_g = norm_final_g if is_last else jnp.ones_like(norm_final_g)
    return _ffndown(act, w_down_bf16, x1, final_g, final_norm=is_last)


def kernel(x, norm_mix_g, w_in, conv_dw_w, conv_dw_b, conv_ln_g, conv_ln_b, lambda_q1, lambda_k1,
           lambda_q2, lambda_k2, subln_g, w_out, norm_ffn_g, w_up, ffn_dw_w, ffn_dw_b, w_down,
           norm_final_g):
    bsz, seq, d = x.shape
    depth = w_in.shape[0]
    x2d = x.reshape(bsz * seq, d)
    for l in range(depth):
        x2d = _layer(x2d, bsz, seq, l, l == depth - 1, norm_final_g, norm_mix_g[l], w_in[l],
                     conv_dw_w[l], conv_dw_b[l], conv_ln_g[l], conv_ln_b[l], lambda_q1[l],
                     lambda_k1[l], lambda_q2[l], lambda_k2[l], subln_g[l], w_out[l],
                     norm_ffn_g[l], w_up[l], ffn_dw_w[l], ffn_dw_b[l], w_down[l])
    return x2d.reshape(bsz, seq, d)
```

```python
import functools
import math

import jax
import jax.numpy as jnp
from jax import lax
from jax.experimental import pallas as pl
from jax.experimental.pallas import tpu as pltpu

F32 = jnp.float32
BF16 = jnp.bfloat16

CHUNK = 64
CONV_K = 31
FFN_CONV_K = 3
ATT_HEADS = 8
ATT_QK_DIM = 64
HEAD_W = 2 * ATT_QK_DIM
NORM_EPS = 1e-6
LN_EPS = 1e-5
MASK_VALUE = -1e30
Q_PRESCALE = math.log2(math.e) / math.sqrt(ATT_QK_DIM)

VMEM_LIMIT_BYTES = 56 * 1024 * 1024
SUBLANES = 8
LANES = 128
BF16_SUBLANES = 16
CONV_HALO = 32


def _params(*semantics):
    return pltpu.CompilerParams(dimension_semantics=semantics,
                                vmem_limit_bytes=VMEM_LIMIT_BYTES)


def _resident(block_shape, index_map):
    return pl.BlockSpec(block_shape, index_map, pipeline_mode=pl.Buffered(1))


def _rmsnorm(x, g):
    ms = jnp.mean(x * x, axis=-1, keepdims=True)
    return x * lax.rsqrt(ms + NORM_EPS) * g


def _sigmoid(x):
    return 1.0 / (1.0 + jnp.exp(-x))


def _inproj_kernel(x_ref, g_ref, wf_ref, side_ref, u_ref, qkv_ref, side_bf16_ref, w_ref, *,
                   row_chunk, n_cast_steps):
    s = pl.program_id(0)

    @pl.when(s < n_cast_steps)
    def _():
        chunk_w = wf_ref.shape[1]
        col = pl.multiple_of(s * chunk_w, chunk_w)
        w_ref[:, pl.ds(col, chunk_w)] = wf_ref[...].astype(BF16)

    @pl.when(s >= n_cast_steps)
    def _():
        side_bf16_ref[...] = side_ref[...].astype(BF16)
        tm, n = u_ref.shape
        n_qkv = qkv_ref.shape[1] // n
        for r0 in range(0, tm, row_chunk):
            rows = slice(r0, r0 + row_chunk)
            h = _rmsnorm(x_ref[rows, :], g_ref[...]).astype(BF16)
            a = jnp.dot(h, w_ref[:, 0:n], preferred_element_type=F32)
            gate = jnp.dot(h, w_ref[:, n:2 * n], preferred_element_type=F32)
            u_ref[rows, :] = a * _sigmoid(gate)
            for c in range(n_qkv):
                z = jnp.dot(h, w_ref[:, (2 + c) * n:(3 + c) * n], preferred_element_type=F32)
                if c == 0:
                    z = z * Q_PRESCALE
                qkv_ref[rows, c * n:(c + 1) * n] = z.astype(BF16)


def _inproj(x2d, g, w_in, conv_ch, side_w, tm=512, row_chunk=256, cast_chunk=512):
    t, d = x2d.shape
    n = conv_ch
    n_groups = w_in.shape[1] // n
    n_tiles = t // tm
    n_cast = w_in.shape[1] // cast_chunk
    side_rows = side_w.shape[0] // n_tiles
    assert w_in.shape[1] == n_groups * n and t % tm == 0 and tm % row_chunk == 0
    assert n_cast * cast_chunk == w_in.shape[1] and cast_chunk % LANES == 0
    assert side_rows * n_tiles == side_w.shape[0] and side_rows % BF16_SUBLANES == 0

    def tile(s):
        return (jnp.maximum(s - n_cast, 0), 0)

    return pl.pallas_call(
        functools.partial(_inproj_kernel, row_chunk=row_chunk, n_cast_steps=n_cast),
        grid=(n_cast + n_tiles,),
        in_specs=[
            pl.BlockSpec((tm, d), tile),
            pl.BlockSpec((1, d), lambda s: (0, 0)),
            pl.BlockSpec((d, cast_chunk), lambda s: (0, jnp.minimum(s, n_cast - 1))),
            pl.BlockSpec((side_rows, side_w.shape[1]), tile),
        ],
        out_specs=[
            pl.BlockSpec((tm, n), tile),
            pl.BlockSpec((tm, (n_groups - 2) * n), tile),
            pl.BlockSpec((side_rows, side_w.shape[1]), tile),
        ],
        out_shape=[
            jax.ShapeDtypeStruct((t, n), F32),
            jax.ShapeDtypeStruct((t, (n_groups - 2) * n), BF16),
            jax.ShapeDtypeStruct(side_w.shape, BF16),
        ],
        scratch_shapes=[pltpu.VMEM((d, n_groups * n), BF16)],
        compiler_params=_params("arbitrary"),
        name="inproj",
    )(x2d, g.reshape(1, d), w_in, side_w)


def _convbranch_kernel(u_ref, halo_ref, w_ref, b_ref, lg_ref, lb_ref, o_ref, win_ref, y_ref, *,
                       row_block):
    t = pl.program_id(1)
    ts, c = u_ref.shape[1], u_ref.shape[2]
    n_chunks = c // LANES
    n_groups = row_block // SUBLANES
    first_tap_row = CONV_HALO - (CONV_K - 1)

    for lc in range(n_chunks):
        lanes = slice(lc * LANES, (lc + 1) * LANES)
        halo = halo_ref[0, :, lanes]
        win_ref[lc, 0:CONV_HALO, :] = jnp.where(t > 0, halo, jnp.zeros_like(halo))
        win_ref[lc, CONV_HALO:, :] = u_ref[0, :, lanes]

    def conv_chunk(lc, carry):
        bias = jnp.broadcast_to(b_ref[lc], (SUBLANES, LANES))
        for r0 in range(0, ts, row_block):
            accs = [bias for _ in range(n_groups)]
            for b in range(SUBLANES):
                taps = list(range(b, CONV_K, SUBLANES))
                wks = [jnp.broadcast_to(w_ref[lc, k:k + 1, :], (SUBLANES, LANES)) for k in taps]
                for j in range(n_groups + len(taps) - 1):
                    row = r0 + first_tap_row + b + SUBLANES * j
                    xw = win_ref[lc, pl.ds(row, SUBLANES), :]
                    for a, wk in enumerate(wks):
                        g = j - a
                        if 0 <= g < n_groups:
                            accs[g] = accs[g] + xw * wk
            for g in range(n_groups):
                y_ref[lc, pl.ds(r0 + SUBLANES * g, SUBLANES), :] = accs[g]
        return carry

    lax.fori_loop(0, n_chunks, conv_chunk, 0)

    y = y_ref[...]
    mu = jnp.sum(jnp.sum(y, axis=0), axis=-1, keepdims=True) * (1.0 / c)
    yc = y - mu
    var = jnp.sum(jnp.sum(yc * yc, axis=0), axis=-1, keepdims=True) * (1.0 / c)
    z = yc * lax.rsqrt(var + LN_EPS) * lg_ref[...] + lb_ref[...]
    act = (z * _sigmoid(z)).astype(BF16)
    for lc in range(n_chunks):
        o_ref[0, :, lc * LANES:(lc + 1) * LANES] = act[lc]


def _convbranch(u, w, b, ln_g, ln_b, ts=1024, row_block=256):
    bsz, seq, c = u.shape
    assert seq % ts == 0 and ts % CONV_HALO == 0 and c % LANES == 0 and w.shape[0] == CONV_K
    assert ts % row_block == 0
    halo_blocks = ts // CONV_HALO
    n_chunks = c // LANES

    def per_chunk(p):
        return p.reshape(-1, n_chunks, LANES).transpose(1, 0, 2)

    def whole(rows):
        return pl.BlockSpec((n_chunks, rows, LANES), lambda bi, ti: (0, 0, 0))

    return pl.pallas_call(
        functools.partial(_convbranch_kernel, row_block=row_block),
        grid=(bsz, seq // ts),
        in_specs=[
            pl.BlockSpec((1, ts, c), lambda bi, ti: (bi, ti, 0)),
            pl.BlockSpec((1, CONV_HALO, c),
                         lambda bi, ti: (bi, jnp.maximum(ti * halo_blocks - 1, 0), 0)),
            whole(CONV_K), whole(1), whole(1), whole(1),
        ],
        out_specs=pl.BlockSpec((1, ts, c), lambda bi, ti: (bi, ti, 0)),
        out_shape=jax.ShapeDtypeStruct((bsz, seq, c), BF16),
        scratch_shapes=[pltpu.VMEM((n_chunks, ts + CONV_HALO, LANES), F32),
                        pltpu.VMEM((n_chunks, ts, LANES), F32)],
        compiler_params=_params("parallel", "parallel"),
        name="convbranch",
    )(u, u, per_chunk(w), per_chunk(b), per_chunk(ln_g), per_chunk(ln_b))


def _attn_kernel(lamp_ref, q_ref, k_ref, v_ref, sg_ref, o_ref, qz_ref, vt_ref, st_ref, smax_ref,
                 m_ref, acc_ref, *, lambda_init, n_ahead):
    qi = pl.program_id(2)
    tq = q_ref.shape[0]
    n_heads, n_blocks, vt_rows, tk = vt_ref.shape
    assert tk == tq

    def head_lanes(g):
        return slice(g * HEAD_W, (g + 1) * HEAD_W)

    @pl.when(qi == 0)
    def _():
        pad_row = lax.broadcasted_iota(jnp.int32, (vt_rows - HEAD_W, tk), 0)
        ones_then_zeros = jnp.where(pad_row == 0, 1.0, 0.0).astype(BF16)
        for g in range(n_heads):
            for j in range(n_blocks):
                vt_ref[g, j, 0:HEAD_W, :] = (
                    v_ref[j * tk:(j + 1) * tk, head_lanes(g)].astype(F32).T.astype(BF16))
                vt_ref[g, j, HEAD_W:, :] = ones_then_zeros

    for g in range(n_heads):
        q = q_ref[:, head_lanes(g)]
        lane = lax.broadcasted_iota(jnp.int32, q.shape, 1)
        zero = jnp.zeros_like(q)
        qz_ref[g, 0:tq, :] = jnp.where(lane < ATT_QK_DIM, q, zero)
        qz_ref[g, tq:, :] = jnp.where(lane >= ATT_QK_DIM, q, zero)
    m_ref[...] = jnp.full(m_ref.shape, -jnp.inf, F32)
    acc_ref[...] = jnp.zeros(acc_ref.shape, F32)

    def scores(j, g, diagonal):
        k0 = pl.multiple_of(j * tk, tk)
        kb = k_ref[pl.ds(k0, tk), head_lanes(g)]
        st = lax.dot_general(kb, qz_ref[g], (((1,), (1,)), ((), ())),
                             preferred_element_type=F32)
        if diagonal is not False:
            key = lax.broadcasted_iota(jnp.int32, st.shape, 0)
            col = lax.broadcasted_iota(jnp.int32, st.shape, 1)
            q_chunk = jnp.where(col >= tq, col - tq, col) // CHUNK
            allowed = key // CHUNK <= q_chunk
            if diagonal is None:
                allowed = jnp.logical_or(allowed, j != qi)
            st = jnp.where(allowed, st, MASK_VALUE)
        return st, jnp.max(st, axis=0, keepdims=True)

    def softmax_pv(j, g, st_and_max):
        st, m_blk = st_and_max
        m_prev = m_ref[g]
        m_new = jnp.maximum(m_prev, m_blk)
        alpha = jnp.exp2(m_prev - m_new)
        p = jnp.exp2(st - m_new)
        acc_ref[g] = alpha * acc_ref[g] + jnp.dot(vt_ref[g, j], p.astype(BF16),
                                                  preferred_element_type=F32)
        m_ref[g] = m_new

    n_slots = st_ref.shape[0]

    def key_block(j, diagonal):
        for g in range(n_heads):
            ahead = g + n_ahead
            if ahead < n_heads:
                st_ref[ahead % n_slots], smax_ref[ahead % n_slots] = scores(j, ahead, diagonal)
            elif not diagonal:
                st_ref[ahead % n_slots], smax_ref[ahead % n_slots] = scores(
                    j + 1, ahead - n_heads, None)
            softmax_pv(j, g, (st_ref[g % n_slots], smax_ref[g % n_slots]))

    def body(j, carry):
        key_block(j, diagonal=False)
        return carry

    for a in range(n_ahead):
        st_ref[a], smax_ref[a] = scores(0, a, None)
    lax.fori_loop(0, qi, body, 0)
    key_block(qi, diagonal=True)

    lp = lamp_ref[...]
    s1 = jnp.sum(lp[0:1] * lp[1:2], axis=-1, keepdims=True)
    s2 = jnp.sum(lp[2:3] * lp[3:4], axis=-1, keepdims=True)
    lam = jnp.exp(s1) - jnp.exp(s2) + lambda_init
    for g in range(n_heads):
        inv_l = 1.0 / acc_ref[g, HEAD_W:HEAD_W + 1, :]
        o = acc_ref[g, 0:HEAD_W, :] * inv_l
        o = o[:, 0:tq] - lam * o[:, tq:]
        ms = jnp.mean(o * o, axis=0, keepdims=True)
        y = (o * lax.rsqrt(ms + NORM_EPS)).T
        o_ref[:, head_lanes(g)] = (y * sg_ref[...] * (1.0 - lambda_init)).astype(BF16)


def _attention(qkv, lam_params, subln_g, bsz, seq, lambda_init, tq=256, heads_per_step=8,
               heads_ahead=2, score_slots=4):
    t = qkv.shape[0]
    n_q_tiles = seq // tq
    gw = heads_per_step * HEAD_W
    n_groups = ATT_HEADS // heads_per_step
    vt_rows = HEAD_W + BF16_SUBLANES
    assert t == bsz * seq and seq % tq == 0 and tq % CHUNK == 0 and ATT_HEADS % heads_per_step == 0
    assert heads_ahead < score_slots and heads_per_step % score_slots == 0
    return pl.pallas_call(
        functools.partial(_attn_kernel, lambda_init=lambda_init, n_ahead=heads_ahead),
        grid=(bsz, n_groups, n_q_tiles),
        in_specs=[
            pl.BlockSpec((4, ATT_QK_DIM), lambda b, hg, qi: (0, 0)),
            pl.BlockSpec((tq, gw), lambda b, hg, qi: (b * n_q_tiles + qi, hg)),
            pl.BlockSpec((seq, gw), lambda b, hg, qi: (b, n_groups + hg)),
            pl.BlockSpec((seq, gw), lambda b, hg, qi: (b, 2 * n_groups + hg)),
            pl.BlockSpec((1, HEAD_W), lambda b, hg, qi: (0, 0)),
        ],
        out_specs=pl.BlockSpec((tq, gw), lambda b, hg, qi: (b * n_q_tiles + qi, hg)),
        out_shape=jax.ShapeDtypeStruct((t, ATT_HEADS * HEAD_W), BF16),
        scratch_shapes=[
            pltpu.VMEM((heads_per_step, 2 * tq, HEAD_W), BF16),
            pltpu.VMEM((heads_per_step, seq // tq, vt_rows, tq), BF16),
            pltpu.VMEM((score_slots, tq, 2 * tq), F32),
            pltpu.VMEM((score_slots, 1, 2 * tq), F32),
            pltpu.VMEM((heads_per_step, 1, 2 * tq), F32),
            pltpu.VMEM((heads_per_step, vt_rows, 2 * tq), F32),
        ],
        compiler_params=_params("parallel", "parallel", "arbitrary"),
        name="diffattn",
    )(lam_params, qkv, qkv, qkv, subln_g.reshape(1, HEAD_W))


def _outproj_kernel(c_ref, a_ref, x_ref, wc_ref, wa_ref, g_ref, x1_ref, h_ref, *, row_chunk):
    for r0 in range(0, x_ref.shape[0], row_chunk):
        rows = slice(r0, r0 + row_chunk)
        mix = jnp.dot(c_ref[rows, :], wc_ref[...], preferred_element_type=F32)
        mix = mix + jnp.dot(a_ref[rows, :], wa_ref[...], preferred_element_type=F32)
        x1 = x_ref[rows, :] + mix
        x1_ref[rows, :] = x1
        h_ref[rows, :] = _rmsnorm(x1, g_ref[...]).astype(BF16)


def _outproj(conv_out, attn_out, x2d, w_out, g, tm=512, row_chunk=256):
    t, d = x2d.shape
    c = conv_out.shape[1]
    a = attn_out.shape[1]
    assert w_out.shape == (c + a, d) and c == a and t % tm == 0 and tm % row_chunk == 0
    return pl.pallas_call(
        functools.partial(_outproj_kernel, row_chunk=row_chunk),
        grid=(t // tm,),
        in_specs=[
            pl.BlockSpec((tm, c), lambda i: (i, 0)),
            pl.BlockSpec((tm, a), lambda i: (i, 0)),
            pl.BlockSpec((tm, d), lambda i: (i, 0)),
            _resident((c, d), lambda i: (0, 0)),
            _resident((a, d), lambda i: (1, 0)),
            pl.BlockSpec((1, d), lambda i: (0, 0)),
        ],
        out_specs=[pl.BlockSpec((tm, d), lambda i: (i, 0)),
                   pl.BlockSpec((tm, d), lambda i: (i, 0))],
        out_shape=[jax.ShapeDtypeStruct((t, d), F32), jax.ShapeDtypeStruct((t, d), BF16)],
        compiler_params=_params("parallel"),
        name="outproj",
    )(conv_out, attn_out, x2d, w_out, w_out, g.reshape(1, d))


def _ffnup_kernel(h_ref, wg_ref, wu_ref, cw_ref, cb_ref, side_ref, o_ref, side_bf16_ref, wgb_ref,
                  wub_ref, gbuf_ref, *, tiles_per_seq, row_chunk):
    i = pl.program_id(1)
    tm, tn = o_ref.shape
    side_bf16_ref[...] = side_ref[...].astype(BF16)

    @pl.when(i == 0)
    def _():
        wgb_ref[...] = wg_ref[...].astype(BF16)
        wub_ref[...] = wu_ref[...].astype(BF16)

    @pl.when(i % tiles_per_seq == 0)
    def _():
        gbuf_ref[:, 0:SUBLANES, :] = jnp.zeros((tn // LANES, SUBLANES, LANES), F32)

    for r0 in range(0, tm, row_chunk):
        rows = slice(r0, r0 + row_chunk)
        h = h_ref[rows, :]
        gate = jnp.dot(h, wgb_ref[...], preferred_element_type=F32)
        up = jnp.dot(h, wub_ref[...], preferred_element_type=F32)
        for lc in range(tn // LANES):
            lanes = slice(lc * LANES, (lc + 1) * LANES)
            g0 = gate[:, lanes]
            gbuf_ref[lc, SUBLANES + r0:SUBLANES + r0 + row_chunk, :] = g0
            g1 = gbuf_ref[lc, pl.ds(SUBLANES + r0 - 1, row_chunk), :]
            g2 = gbuf_ref[lc, pl.ds(SUBLANES + r0 - 2, row_chunk), :]
            conv = (cb_ref[:, lanes] + cw_ref[2:3, lanes] * g0 + cw_ref[1:2, lanes] * g1
                    + cw_ref[0:1, lanes] * g2)
            gelu = 0.5 * conv * (1.0 + lax.erf(conv * math.sqrt(0.5)))
            o_ref[rows, lanes] = (gelu * up[:, lanes]).astype(BF16)
    gbuf_ref[:, 0:SUBLANES, :] = gbuf_ref[:, tm:tm + SUBLANES, :]


def _ffnup(h2, w_up, conv_w, conv_b, seq, side_w, tm=2048, tn=512, row_chunk=256):
    t, d = h2.shape
    d_ff = w_up.shape[1] // 2
    assert t % tm == 0 and seq % tm == 0 and d_ff % tn == 0 and conv_w.shape[0] == FFN_CONV_K
    assert tm % row_chunk == 0 and tn % LANES == 0
    n_col = d_ff // tn
    n_tok = t // tm
    side_rows = side_w.shape[0] // (n_col * n_tok)
    assert side_rows * n_col * n_tok == side_w.shape[0] and side_rows % BF16_SUBLANES == 0
    side_spec = pl.BlockSpec((side_rows, side_w.shape[1]), lambda j, i: (j * n_tok + i, 0))
    return pl.pallas_call(
        functools.partial(_ffnup_kernel, tiles_per_seq=seq // tm, row_chunk=row_chunk),
        grid=(n_col, n_tok),
        in_specs=[
            pl.BlockSpec((tm, d), lambda j, i: (i, 0)),
            pl.BlockSpec((d, tn), lambda j, i: (0, j)),
            pl.BlockSpec((d, tn), lambda j, i: (0, n_col + j)),
            pl.BlockSpec((FFN_CONV_K, tn), lambda j, i: (0, j)),
            pl.BlockSpec((1, tn), lambda j, i: (0, j)),
            side_spec,
        ],
        out_specs=[pl.BlockSpec((tm, tn), lambda j, i: (i, j)), side_spec],
        out_shape=[jax.ShapeDtypeStruct((t, d_ff), BF16),
                   jax.ShapeDtypeStruct(side_w.shape, BF16)],
        scratch_shapes=[pltpu.VMEM((d, tn), BF16), pltpu.VMEM((d, tn), BF16),
                        pltpu.VMEM((tn // LANES, tm + SUBLANES, LANES), F32)],
        compiler_params=_params("parallel", "arbitrary"),
        name="ffnup",
    )(h2, w_up, w_up, conv_w, conv_b.reshape(1, d_ff), side_w)


def _ffndown_kernel(a_ref, w_ref, x1_ref, g_ref, o_ref, *, final_norm, row_chunk):
    for r0 in range(0, a_ref.shape[0], row_chunk):
        rows = slice(r0, r0 + row_chunk)
        x2 = x1_ref[rows, :] + jnp.dot(a_ref[rows, :], w_ref[...], preferred_element_type=F32)
        o_ref[rows, :] = _rmsnorm(x2, g_ref[...]) if final_norm else x2


def _ffndown(act, w_down, x1, g, final_norm, tm=512, row_chunk=256):
    t, d_ff = act.shape
    d = w_down.shape[1]
    assert t % tm == 0 and tm % row_chunk == 0
    return pl.pallas_call(
        functools.partial(_ffndown_kernel, final_norm=final_norm, row_chunk=row_chunk),
        grid=(t // tm,),
        in_specs=[
            pl.BlockSpec((tm, d_ff), lambda i: (i, 0)),
            _resident((d_ff, d), lambda i: (0, 0)),
            pl.BlockSpec((tm, d), lambda i: (i, 0)),
            pl.BlockSpec((1, d), lambda i: (0, 0)),
        ],
        out_specs=pl.BlockSpec((tm, d), lambda i: (i, 0)),
        out_shape=jax.ShapeDtypeStruct((t, d), F32),
        compiler_params=_params("parallel"),
        name="ffndown",
    )(act, w_down, x1, g.reshape(1, d))


def _layer(x2d, bsz, seq, layer_idx, is_last, norm_final_g, norm_mix_g, w_in, conv_dw_w,
           conv_dw_b, conv_ln_g, conv_ln_b, lambda_q1, lambda_k1, lambda_q2, lambda_k2, subln_g,
           w_out, norm_ffn_g, w_up, ffn_dw_w, ffn_dw_b, w_down):
    conv_ch = conv_dw_w.shape[1]
    lambda_init = 0.8 - 0.6 * math.exp(-0.3 * layer_idx)
    u, qkv, w_out_bf16 = _inproj(x2d, norm_mix_g, w_in, conv_ch, w_out)
    conv_out = _convbranch(u.reshape(bsz, seq, conv_ch), conv_dw_w, conv_dw_b, conv_ln_g,
                           conv_ln_b).reshape(bsz * seq, conv_ch)
    lam_params = jnp.stack([lambda_q1, lambda_k1, lambda_q2, lambda_k2]).astype(F32)
    attn_out = _attention(qkv, lam_params, subln_g, bsz, seq, lambda_init)
    x1, h2 = _outproj(conv_out, attn_out, x2d, w_out_bf16, norm_ffn_g)
    act, w_down_bf16 = _ffnup(h2, w_up, ffn_dw_w, ffn_dw_b, seq, w_down)
    final_g = norm_final_g if is_last else jnp.ones_like(norm_final_g)
    return _ffndown(act, w_down_bf16, x1, final_g, final_norm=is_last)


def kernel(x, norm_mix_g, w_in, conv_dw_w, conv_dw_b, conv_ln_g, conv_ln_b, lambda_q1, lambda_k1,
           lambda_q2, lambda_k2, subln_g, w_out, norm_ffn_g, w_up, ffn_dw_w, ffn_dw_b, w_down,
           norm_final_g):
    bsz, seq, d = x.shape
    depth = w_in.shape[0]
    x2d = x.reshape(bsz * seq, d)
    for l in range(depth):
        x2d = _layer(x2d, bsz, seq, l, l == depth - 1, norm_final_g, norm_mix_g[l], w_in[l],
                     conv_dw_w[l], conv_dw_b[l], conv_ln_g[l], conv_ln_b[l], lambda_q1[l],
                     lambda_k1[l], lambda_q2[l], lambda_k2[l], subln_g[l], w_out[l],
                     norm_ffn_g[l], w_up[l], ffn_dw_w[l], ffn_dw_b[l], w_down[l])
    return x2d.reshape(bsz, seq, d)
```

```python
import functools
import math

import jax
import jax.numpy as jnp
from jax import lax
from jax.experimental import pallas as pl
from jax.experimental.pallas import tpu as pltpu

F32 = jnp.float32
BF16 = jnp.bfloat16

CHUNK = 64
CONV_K = 31
FFN_CONV_K = 3
ATT_HEADS = 8
ATT_QK_DIM = 64
HEAD_W = 2 * ATT_QK_DIM
NORM_EPS = 1e-6
LN_EPS = 1e-5
MASK_VALUE = -1e30
Q_PRESCALE = math.log2(math.e) / math.sqrt(ATT_QK_DIM)

VMEM_LIMIT_BYTES = 56 * 1024 * 1024
SUBLANES = 8
LANES = 128
BF16_SUBLANES = 16
CONV_HALO = 32


def _params(*semantics):
    return pltpu.CompilerParams(dimension_semantics=semantics,
                                vmem_limit_bytes=VMEM_LIMIT_BYTES)


def _resident(block_shape, index_map):
    return pl.BlockSpec(block_shape, index_map, pipeline_mode=pl.Buffered(1))


def _rmsnorm(x, g):
    ms = jnp.mean(x * x, axis=-1, keepdims=True)
    return x * lax.rsqrt(ms + NORM_EPS) * g


def _sigmoid(x):
    return 1.0 / (1.0 + jnp.exp(-x))


def _inproj_kernel(x_ref, g_ref, wf_ref, side_ref, u_ref, qk_ref, vt_ref, side_bf16_ref, w_ref, *,
                   row_chunk, n_cast_steps):
    s = pl.program_id(0)

    @pl.when(s < n_cast_steps)
    def _():
        chunk_w = wf_ref.shape[1]
        col = pl.multiple_of(s * chunk_w, chunk_w)
        w_ref[:, pl.ds(col, chunk_w)] = wf_ref[...].astype(BF16)

    @pl.when(s >= n_cast_steps)
    def _():
        side_bf16_ref[...] = side_ref[...].astype(BF16)
        tm, n = u_ref.shape
        for r0 in range(0, tm, row_chunk):
            rows = slice(r0, r0 + row_chunk)
            h = _rmsnorm(x_ref[rows, :], g_ref[...]).astype(BF16)
            a = jnp.dot(h, w_ref[:, 0:n], preferred_element_type=F32)
            gate = jnp.dot(h, w_ref[:, n:2 * n], preferred_element_type=F32)
            u_ref[rows, :] = a * _sigmoid(gate)
            q = jnp.dot(h, w_ref[:, 2 * n:3 * n], preferred_element_type=F32)
            qk_ref[rows, 0:n] = (q * Q_PRESCALE).astype(BF16)
            k = jnp.dot(h, w_ref[:, 3 * n:4 * n], preferred_element_type=F32)
            qk_ref[rows, n:2 * n] = k.astype(BF16)
            v = jnp.dot(h, w_ref[:, 4 * n:5 * n], preferred_element_type=F32)
            vt_ref[:, rows] = v.T.astype(BF16)


def _inproj(x2d, g, w_in, conv_ch, side_w, tm=512, row_chunk=256, cast_chunk=512):
    t, d = x2d.shape
    n = conv_ch
    n_groups = w_in.shape[1] // n
    assert n_groups == 5
    n_tiles = t // tm
    n_cast = w_in.shape[1] // cast_chunk
    side_rows = side_w.shape[0] // n_tiles
    assert w_in.shape[1] == n_groups * n and t % tm == 0 and tm % row_chunk == 0
    assert n_cast * cast_chunk == w_in.shape[1] and cast_chunk % LANES == 0
    assert side_rows * n_tiles == side_w.shape[0] and side_rows % BF16_SUBLANES == 0

    def tile(s):
        return (jnp.maximum(s - n_cast, 0), 0)

    return pl.pallas_call(
        functools.partial(_inproj_kernel, row_chunk=row_chunk, n_cast_steps=n_cast),
        grid=(n_cast + n_tiles,),
        in_specs=[
            pl.BlockSpec((tm, d), tile),
            pl.BlockSpec((1, d), lambda s: (0, 0)),
            pl.BlockSpec((d, cast_chunk), lambda s: (0, jnp.minimum(s, n_cast - 1))),
            pl.BlockSpec((side_rows, side_w.shape[1]), tile),
        ],
        out_specs=[
            pl.BlockSpec((tm, n), tile),
            pl.BlockSpec((tm, 2 * n), tile),
            pl.BlockSpec((n, tm), lambda s: (0, jnp.maximum(s - n_cast, 0))),
            pl.BlockSpec((side_rows, side_w.shape[1]), tile),
        ],
        out_shape=[
            jax.ShapeDtypeStruct((t, n), F32),
            jax.ShapeDtypeStruct((t, 2 * n), BF16),
            jax.ShapeDtypeStruct((n, t), BF16),
            jax.ShapeDtypeStruct(side_w.shape, BF16),
        ],
        scratch_shapes=[pltpu.VMEM((d, n_groups * n), BF16)],
        compiler_params=_params("arbitrary"),
        name="inproj",
    )(x2d, g.reshape(1, d), w_in, side_w)


def _convbranch_kernel(u_ref, halo_ref, w_ref, b_ref, lg_ref, lb_ref, o_ref, win_ref, y_ref, *,
                       row_block):
    t = pl.program_id(1)
    ts, c = u_ref.shape[1], u_ref.shape[2]
    n_chunks = c // LANES
    n_groups = row_block // SUBLANES
    first_tap_row = CONV_HALO - (CONV_K - 1)

    for lc in range(n_chunks):
        lanes = slice(lc * LANES, (lc + 1) * LANES)
        halo = halo_ref[0, :, lanes]
        win_ref[lc, 0:CONV_HALO, :] = jnp.where(t > 0, halo, jnp.zeros_like(halo))
        win_ref[lc, CONV_HALO:, :] = u_ref[0, :, lanes]

    def conv_chunk(lc, carry):
        bias = jnp.broadcast_to(b_ref[lc], (SUBLANES, LANES))
        for r0 in range(0, ts, row_block):
            accs = [bias for _ in range(n_groups)]
            for b in range(SUBLANES):
                taps = list(range(b, CONV_K, SUBLANES))
                wks = [jnp.broadcast_to(w_ref[lc, k:k + 1, :], (SUBLANES, LANES)) for k in taps]
                for j in range(n_groups + len(taps) - 1):
                    row = r0 + first_tap_row + b + SUBLANES * j
                    xw = win_ref[lc, pl.ds(row, SUBLANES), :]
                    for a, wk in enumerate(wks):
                        g = j - a
                        if 0 <= g < n_groups:
                            accs[g] = accs[g] + xw * wk
            for g in range(n_groups):
                y_ref[lc, pl.ds(r0 + SUBLANES * g, SUBLANES), :] = accs[g]
        return carry

    lax.fori_loop(0, n_chunks, conv_chunk, 0)

    y = y_ref[...]
    mu = jnp.sum(jnp.sum(y, axis=0), axis=-1, keepdims=True) * (1.0 / c)
    yc = y - mu
    var = jnp.sum(jnp.sum(yc * yc, axis=0), axis=-1, keepdims=True) * (1.0 / c)
    z = yc * lax.rsqrt(var + LN_EPS) * lg_ref[...] + lb_ref[...]
    act = (z * _sigmoid(z)).astype(BF16)
    for lc in range(n_chunks):
        o_ref[0, :, lc * LANES:(lc + 1) * LANES] = act[lc]


def _convbranch(u, w, b, ln_g, ln_b, ts=1024, row_block=256):
    bsz, seq, c = u.shape
    assert seq % ts == 0 and ts % CONV_HALO == 0 and c % LANES == 0 and w.shape[0] == CONV_K
    assert ts % row_block == 0
    halo_blocks = ts // CONV_HALO
    n_chunks = c // LANES

    def per_chunk(p):
        return p.reshape(-1, n_chunks, LANES).transpose(1, 0, 2)

    def whole(rows):
        return pl.BlockSpec((n_chunks, rows, LANES), lambda bi, ti: (0, 0, 0))

    return pl.pallas_call(
        functools.partial(_convbranch_kernel, row_block=row_block),
        grid=(bsz, seq // ts),
        in_specs=[
            pl.BlockSpec((1, ts, c), lambda bi, ti: (bi, ti, 0)),
            pl.BlockSpec((1, CONV_HALO, c),
                         lambda bi, ti: (bi, jnp.maximum(ti * halo_blocks - 1, 0), 0)),
            whole(CONV_K), whole(1), whole(1), whole(1),
        ],
        out_specs=pl.BlockSpec((1, ts, c), lambda bi, ti: (bi, ti, 0)),
        out_shape=jax.ShapeDtypeStruct((bsz, seq, c), BF16),
        scratch_shapes=[pltpu.VMEM((n_chunks, ts + CONV_HALO, LANES), F32),
                        pltpu.VMEM((n_chunks, ts, LANES), F32)],
        compiler_params=_params("parallel", "parallel"),
        name="convbranch",
    )(u, u, per_chunk(w), per_chunk(b), per_chunk(ln_g), per_chunk(ln_b))


def _attn_kernel(lamp_ref, q_ref, k_ref, vsrc_ref, sg_ref, o_ref, qz_ref, vt_ref, st_ref, smax_ref,
                 m_ref, acc_ref, *, lambda_init, n_ahead):
    qi = pl.program_id(2)
    tq = q_ref.shape[0]
    n_heads, n_blocks, vt_rows, tk = vt_ref.shape
    assert tk == tq

    def head_lanes(g):
        return slice(g * HEAD_W, (g + 1) * HEAD_W)

    @pl.when(qi == 0)
    def _():
        pad_row = lax.broadcasted_iota(jnp.int32, (vt_rows - HEAD_W, tk), 0)
        ones_then_zeros = jnp.where(pad_row == 0, 1.0, 0.0).astype(BF16)
        for g in range(n_heads):
            for j in range(n_blocks):
                vt_ref[g, j, 0:HEAD_W, :] = vsrc_ref[head_lanes(g), j * tk:(j + 1) * tk]
                vt_ref[g, j, HEAD_W:, :] = ones_then_zeros

    for g in range(n_heads):
        q = q_ref[:, head_lanes(g)]
        lane = lax.broadcasted_iota(jnp.int32, q.shape, 1)
        zero = jnp.zeros_like(q)
        qz_ref[g, 0:tq, :] = jnp.where(lane < ATT_QK_DIM, q, zero)
        qz_ref[g, tq:, :] = jnp.where(lane >= ATT_QK_DIM, q, zero)
    m_ref[...] = jnp.full(m_ref.shape, -jnp.inf, F32)
    acc_ref[...] = jnp.zeros(acc_ref.shape, F32)

    def scores(j, g, diagonal):
        k0 = pl.multiple_of(j * tk, tk)
        kb = k_ref[pl.ds(k0, tk), head_lanes(g)]
        st = lax.dot_general(kb, qz_ref[g], (((1,), (1,)), ((), ())),
                             preferred_element_type=F32)
        if diagonal is not False:
            key = lax.broadcasted_iota(jnp.int32, st.shape, 0)
            col = lax.broadcasted_iota(jnp.int32, st.shape, 1)
            q_chunk = jnp.where(col >= tq, col - tq, col) // CHUNK
            allowed = key // CHUNK <= q_chunk
            if diagonal is None:
                allowed = jnp.logical_or(allowed, j != qi)
            st = jnp.where(allowed, st, MASK_VALUE)
        return st, jnp.max(st, axis=0, keepdims=True)

    def softmax_pv(j, g, st_and_max):
        st, m_blk = st_and_max
        m_prev = m_ref[g]
        m_new = jnp.maximum(m_prev, m_blk)
        alpha = jnp.exp2(m_prev - m_new)
        p = jnp.exp2(st - m_new)
        acc_ref[g] = alpha * acc_ref[g] + jnp.dot(vt_ref[g, j], p.astype(BF16),
                                                  preferred_element_type=F32)
        m_ref[g] = m_new

    n_slots = st_ref.shape[0]

    def key_block(j, diagonal):
        for g in range(n_heads):
            ahead = g + n_ahead
            if ahead < n_heads:
                st_ref[ahead % n_slots], smax_ref[ahead % n_slots] = scores(j, ahead, diagonal)
            elif not diagonal:
                st_ref[ahead % n_slots], smax_ref[ahead % n_slots] = scores(
                    j + 1, ahead - n_heads, None)
            softmax_pv(j, g, (st_ref[g % n_slots], smax_ref[g % n_slots]))

    def body(j, carry):
        key_block(j, diagonal=False)
        return carry

    for a in range(n_ahead):
        st_ref[a], smax_ref[a] = scores(0, a, None)
    lax.fori_loop(0, qi, body, 0)
    key_block(qi, diagonal=True)

    lp = lamp_ref[...]
    s1 = jnp.sum(lp[0:1] * lp[1:2], axis=-1, keepdims=True)
    s2 = jnp.sum(lp[2:3] * lp[3:4], axis=-1, keepdims=True)
    lam = jnp.exp(s1) - jnp.exp(s2) + lambda_init
    for g in range(n_heads):
        inv_l = 1.0 / acc_ref[g, HEAD_W:HEAD_W + 1, :]
        o = acc_ref[g, 0:HEAD_W, :] * inv_l
        o = o[:, 0:tq] - lam * o[:, tq:]
        ms = jnp.mean(o * o, axis=0, keepdims=True)
        y = (o * lax.rsqrt(ms + NORM_EPS)).T
        o_ref[:, head_lanes(g)] = (y * sg_ref[...] * (1.0 - lambda_init)).astype(BF16)


def _attention(qk, v_t, lam_params, subln_g, bsz, seq, lambda_init, tq=256, heads_per_step=8,
               heads_ahead=2, score_slots=4):
    t = qk.shape[0]
    n_q_tiles = seq // tq
    gw = heads_per_step * HEAD_W
    n_groups = ATT_HEADS // heads_per_step
    vt_rows = HEAD_W + BF16_SUBLANES
    assert t == bsz * seq and seq % tq == 0 and tq % CHUNK == 0 and ATT_HEADS % heads_per_step == 0
    assert heads_ahead < score_slots and heads_per_step % score_slots == 0
    return pl.pallas_call(
        functools.partial(_attn_kernel, lambda_init=lambda_init, n_ahead=heads_ahead),
        grid=(bsz, n_groups, n_q_tiles),
        in_specs=[
            pl.BlockSpec((4, ATT_QK_DIM), lambda b, hg, qi: (0, 0)),
            pl.BlockSpec((tq, gw), lambda b, hg, qi: (b * n_q_tiles + qi, hg)),
            pl.BlockSpec((seq, gw), lambda b, hg, qi: (b, n_groups + hg)),
            pl.BlockSpec((gw, seq), lambda b, hg, qi: (hg, b)),
            pl.BlockSpec((1, HEAD_W), lambda b, hg, qi: (0, 0)),
        ],
        out_specs=pl.BlockSpec((tq, gw), lambda b, hg, qi: (b * n_q_tiles + qi, hg)),
        out_shape=jax.ShapeDtypeStruct((t, ATT_HEADS * HEAD_W), BF16),
        scratch_shapes=[
            pltpu.VMEM((heads_per_step, 2 * tq, HEAD_W), BF16),
            pltpu.VMEM((heads_per_step, seq // tq, vt_rows, tq), BF16),
            pltpu.VMEM((score_slots, tq, 2 * tq), F32),
            pltpu.VMEM((score_slots, 1, 2 * tq), F32),
            pltpu.VMEM((heads_per_step, 1, 2 * tq), F32),
            pltpu.VMEM((heads_per_step, vt_rows, 2 * tq), F32),
        ],
        compiler_params=_params("parallel", "parallel", "arbitrary"),
        name="diffattn",
    )(lam_params, qk, qk, v_t, subln_g.reshape(1, HEAD_W))


def _outproj_kernel(c_ref, a_ref, x_ref, wc_ref, wa_ref, g_ref, x1_ref, h_ref, *, row_chunk):
    for r0 in range(0, x_ref.shape[0], row_chunk):
        rows = slice(r0, r0 + row_chunk)
        mix = jnp.dot(c_ref[rows, :], wc_ref[...], preferred_element_type=F32)
        mix = mix + jnp.dot(a_ref[rows, :], wa_ref[...], preferred_element_type=F32)
        x1 = x_ref[rows, :] + mix
        x1_ref[rows, :] = x1
        h_ref[rows, :] = _rmsnorm(x1, g_ref[...]).astype(BF16)


def _outproj(conv_out, attn_out, x2d, w_out, g, tm=512, row_chunk=256):
    t, d = x2d.shape
    c = conv_out.shape[1]
    a = attn_out.shape[1]
    assert w_out.shape == (c + a, d) and c == a and t % tm == 0 and tm % row_chunk == 0
    return pl.pallas_call(
        functools.partial(_outproj_kernel, row_chunk=row_chunk),
        grid=(t // tm,),
        in_specs=[
            pl.BlockSpec((tm, c), lambda i: (i, 0)),
            pl.BlockSpec((tm, a), lambda i: (i, 0)),
            pl.BlockSpec((tm, d), lambda i: (i, 0)),
            _resident((c, d), lambda i: (0, 0)),
            _resident((a, d), lambda i: (1, 0)),
            pl.BlockSpec((1, d), lambda i: (0, 0)),
        ],
        out_specs=[pl.BlockSpec((tm, d), lambda i: (i, 0)),
                   pl.BlockSpec((tm, d), lambda i: (i, 0))],
        out_shape=[jax.ShapeDtypeStruct((t, d), F32), jax.ShapeDtypeStruct((t, d), BF16)],
        compiler_params=_params("parallel"),
        name="outproj",
    )(conv_out, attn_out, x2d, w_out, w_out, g.reshape(1, d))


def _ffnup_kernel(h_ref, wg_ref, wu_ref, cw_ref, cb_ref, side_ref, o_ref, side_bf16_ref, wgb_ref,
                  wub_ref, gbuf_ref, *, tiles_per_seq, row_chunk):
    i = pl.program_id(1)
    tm, tn = o_ref.shape
    side_bf16_ref[...] = side_ref[...].astype(BF16)

    @pl.when(i == 0)
    def _():
        wgb_ref[...] = wg_ref[...].astype(BF16)
        wub_ref[...] = wu_ref[...].astype(BF16)

    @pl.when(i % tiles_per_seq == 0)
    def _():
        gbuf_ref[:, 0:SUBLANES, :] = jnp.zeros((tn // LANES, SUBLANES, LANES), F32)

    for r0 in range(0, tm, row_chunk):
        rows = slice(r0, r0 + row_chunk)
        h = h_ref[rows, :]
        gate = jnp.dot(h, wgb_ref[...], preferred_element_type=F32)
        up = jnp.dot(h, wub_ref[...], preferred_element_type=F32)
        for lc in range(tn // LANES):
            lanes = slice(lc * LANES, (lc + 1) * LANES)
            g0 = gate[:, lanes]
            gbuf_ref[lc, SUBLANES + r0:SUBLANES + r0 + row_chunk, :] = g0
            g1 = gbuf_ref[lc, pl.ds(SUBLANES + r0 - 1, row_chunk), :]
            g2 = gbuf_ref[lc, pl.ds(SUBLANES + r0 - 2, row_chunk), :]
            conv = (cb_ref[:, lanes] + cw_ref[2:3, lanes] * g0 + cw_ref[1:2, lanes] * g1
                    + cw_ref[0:1, lanes] * g2)
            gelu = 0.5 * conv * (1.0 + lax.erf(conv * math.sqrt(0.5)))
            o_ref[rows, lanes] = (gelu * up[:, lanes]).astype(BF16)
    gbuf_ref[:, 0:SUBLANES, :] = gbuf_ref[:, tm:tm + SUBLANES, :]


def _ffnup(h2, w_up, conv_w, conv_b, seq, side_w, tm=2048, tn=512, row_chunk=256):
    t, d = h2.shape
    d_ff = w_up.shape[1] // 2
    assert t % tm == 0 and seq % tm == 0 and d_ff % tn == 0 and conv_w.shape[0] == FFN_CONV_K
    assert tm % row_chunk == 0 and tn % LANES == 0
    n_col = d_ff // tn
    n_tok = t // tm
    side_rows = side_w.shape[0] // (n_col * n_tok)
    assert side_rows * n_col * n_tok == side_w.shape[0] and side_rows % BF16_SUBLANES == 0
    side_spec = pl.BlockSpec((side_rows, side_w.shape[1]), lambda j, i: (j * n_tok + i, 0))
    return pl.pallas_call(
        functools.partial(_ffnup_kernel, tiles_per_seq=seq // tm, row_chunk=row_chunk),
        grid=(n_col, n_tok),
        in_specs=[
            pl.BlockSpec((tm, d), lambda j, i: (i, 0)),
            pl.BlockSpec((d, tn), lambda j, i: (0, j)),
            pl.BlockSpec((d, tn), lambda j, i: (0, n_col + j)),
            pl.BlockSpec((FFN_CONV_K, tn), lambda j, i: (0, j)),
            pl.BlockSpec((1, tn), lambda j, i: (0, j)),
            side_spec,
        ],
        out_specs=[pl.BlockSpec((tm, tn), lambda j, i: (i, j)), side_spec],
        out_shape=[jax.ShapeDtypeStruct((t, d_ff), BF16),
                   jax.ShapeDtypeStruct(side_w.shape, BF16)],
        scratch_shapes=[pltpu.VMEM((d, tn), BF16), pltpu.VMEM((d, tn), BF16),
                        pltpu.VMEM((tn // LANES, tm + SUBLANES, LANES), F32)],
        compiler_params=_params("parallel", "arbitrary"),
        name="ffnup",
    )(h2, w_up, w_up, conv_w, conv_b.reshape(1, d_ff), side_w)


def _ffndown_kernel(a_ref, w_ref, x1_ref, g_ref, o_ref, *, final_norm, row_chunk):
    for r0 in range(0, a_ref.shape[0], row_chunk):
        rows = slice(r0, r0 + row_chunk)
        x2 = x1_ref[rows, :] + jnp.dot(a_ref[rows, :], w_ref[...], preferred_element_type=F32)
        o_ref[rows, :] = _rmsnorm(x2, g_ref[...]) if final_norm else x2


def _ffndown(act, w_down, x1, g, final_norm, tm=512, row_chunk=256):
    t, d_ff = act.shape
    d = w_down.shape[1]
    assert t % tm == 0 and tm % row_chunk == 0
    return pl.pallas_call(
        functools.partial(_ffndown_kernel, final_norm=final_norm, row_chunk=row_chunk),
        grid=(t // tm,),
        in_specs=[
            pl.BlockSpec((tm, d_ff), lambda i: (i, 0)),
            _resident((d_ff, d), lambda i: (0, 0)),
            pl.BlockSpec((tm, d), lambda i: (i, 0)),
            pl.BlockSpec((1, d), lambda i: (0, 0)),
        ],
        out_specs=pl.BlockSpec((tm, d), lambda i: (i, 0)),
        out_shape=jax.ShapeDtypeStruct((t, d), F32),
        compiler_params=_params("parallel"),
        name="ffndown",
    )(act, w_down, x1, g.reshape(1, d))


def _layer(x2d, bsz, seq, layer_idx, is_last, norm_final_g, norm_mix_g, w_in, conv_dw_w,
           conv_dw_b, conv_ln_g, conv_ln_b, lambda_q1, lambda_k1, lambda_q2, lambda_k2, subln_g,
           w_out, norm_ffn_g, w_up, ffn_dw_w, ffn_dw_b, w_down):
    conv_ch = conv_dw_w.shape[1]
    lambda_init = 0.8 - 0.6 * math.exp(-0.3 * layer_idx)
    u, qk, v_t, w_out_bf16 = _inproj(x2d, norm_mix_g, w_in, conv_ch, w_out)
    conv_out = _convbranch(u.reshape(bsz, seq, conv_ch), conv_dw_w, conv_dw_b, conv_ln_g,
                           conv_ln_b).reshape(bsz * seq, conv_ch)
    lam_params = jnp.stack([lambda_q1, lambda_k1, lambda_q2, lambda_k2]).astype(F32)
    attn_out = _attention(qk, v_t, lam_params, subln_g, bsz, seq, lambda_init)
    x1, h2 = _outproj(conv_out, attn_out, x2d, w_out_bf16, norm_ffn_g)
    act, w_down_bf16 = _ffnup(h2, w_up, ffn_dw_w, ffn_dw_b, seq, w_down)
    final_g = norm_final_g if is_last else jnp.ones_like(norm_final_g)
    return _ffndown(act, w_down_bf16, x1, final_g, final_norm=is_last)


def kernel(x, norm_mix_g, w_in, conv_dw_w, conv_dw_b, conv_ln_g, conv_ln_b, lambda_q1, lambda_k1,
           lambda_q2, lambda_k2, subln_g, w_out, norm_ffn_g, w_up, ffn_dw_w, ffn_dw_b, w_down,
           norm_final_g):
    bsz, seq, d = x.shape
    depth = w_in.shape[0]
    x2d = x.reshape(bsz * seq, d)
    for l in range(depth):
        x2d = _layer(x2d, bsz, seq, l, l == depth - 1, norm_final_g, norm_mix_g[l], w_in[l],
                     conv_dw_w[l], conv_dw_b[l], conv_ln_g[l], conv_ln_b[l], lambda_q1[l],
                     lambda_k1[l], lambda_q2[l], lambda_k2[l], subln_g[l], w_out[l],
                     norm_ffn_g[l], w_up[l], ffn_dw_w[l], ffn_dw_b[l], w_down[l])
    return x2d.reshape(bsz, seq, d)
```
